```python
import math
import jax
import jax.numpy as jnp
from jax import lax
import numpy as np

D_MODEL = 1024
BATCH = 8
SEQ = 2048
DEPTH = 4
DEC_BATCH = 128
DEC_SEQ = 4
PAST_LEN = 8192
PAGE_SIZE = 128

D_MIX = D_MODEL
GDN_H = 4
GDN_DK = 128
GDN_DV = 128
GDN_CONV = 4
GDN_CHUNK = 64
GDN_QK = GDN_H * GDN_DK
GDN_VW = GDN_H * GDN_DV
GDN_CONV_CH = 2 * GDN_QK + GDN_VW
MLA_H = 4
MLA_Q_LORA = 384
MLA_KV_LORA = 256
MLA_NOPE = 128
MLA_ROPE = 64
MLA_V = 128
ROPE_THETA = 10000.0
Q_BLOCK = 128
D_IN = GDN_CONV_CH + GDN_VW + 2 * GDN_H + MLA_Q_LORA + MLA_KV_LORA + MLA_ROPE
N_MEM = 256
MEM_H = 4
MEM_DH = 128
PEER_H = 8
PEER_N_KEYS = 128
PEER_N_EXPERTS = PEER_N_KEYS * PEER_N_KEYS
PEER_DQ = 128
PEER_TOPK = 16
PEER_BLOCK = 256
EPS = 1e-6

kernel_name = 'hymba_gdn_mla_peer_decode_step'


def rmsnorm(x, g):
    xf = x.astype(jnp.float32)
    y = xf * lax.rsqrt(jnp.mean(xf * xf, axis=-1, keepdims=True) + EPS)
    return (y * g.astype(jnp.float32)).astype(x.dtype)


def l2norm(x):
    xf = x.astype(jnp.float32)
    return (xf * lax.rsqrt(jnp.sum(xf * xf, axis=-1, keepdims=True) + EPS)).astype(x.dtype)


def rope_tables(pos):
    half = MLA_ROPE // 2
    inv = jnp.exp(-math.log(ROPE_THETA) * jnp.arange(half, dtype=jnp.float32) / half)
    ang = pos.astype(jnp.float32)[:, None] * inv[None, :]
    return jnp.cos(ang), jnp.sin(ang)


def apply_rope(x, cos, sin):
    xf = x.astype(jnp.float32)
    x1, x2 = jnp.split(xf, 2, axis=-1)
    return jnp.concatenate([x1 * cos - x2 * sin, x2 * cos + x1 * sin], axis=-1).astype(x.dtype)


def causal_conv(x, buf, w):
    t = x.shape[1]
    xp = jnp.concatenate([buf.astype(x.dtype), x], axis=1)
    y = xp[:, 0:t] * w[0]
    for i in range(1, GDN_CONV):
        y = y + xp[:, i:i + t] * w[i]
    return jax.nn.silu(y), xp[:, t:]


def gated_delta_chunked(q, k, v, g, beta, s0):
    b, t, nh, dk = k.shape
    dv = v.shape[-1]
    c = GDN_CHUNK if t >= GDN_CHUNK else t
    pad = (-t) % c
    n = (t + pad) // c

    def to_chunks(a):
        a = a.astype(jnp.float32)
        a = jnp.pad(a, [(0, 0), (0, pad)] + [(0, 0)] * (a.ndim - 2))
        a = a.reshape((b, n, c) + a.shape[2:])
        return jnp.moveaxis(a, 3, 1)

    q, k, v, g, beta = (to_chunks(a) for a in (q, k, v, g, beta))
    G = jnp.cumsum(g, axis=-1)
    causal = jnp.tril(jnp.ones((c, c), dtype=bool))
    strict = jnp.tril(jnp.ones((c, c), dtype=bool), -1)
    diff = G[..., :, None] - G[..., None, :]
    decay = jnp.where(causal, jnp.exp(jnp.where(causal, diff, 0.0)), 0.0)
    k_beta = k * beta[..., None]
    lower = jnp.where(strict, jnp.einsum('bhnid,bhnjd->bhnij', k_beta, k) * decay, 0.0)
    eye = jnp.eye(c, dtype=jnp.float32)
    t_inv = lax.linalg.triangular_solve(eye + lower, jnp.broadcast_to(eye, lower.shape),
                                        left_side=True, lower=True)
    u = jnp.einsum('bhnij,bhnje->bhnie', t_inv, v * beta[..., None])
    w = jnp.einsum('bhnij,bhnjd->bhnid', t_inv, k_beta * jnp.exp(G)[..., None])
    qk = jnp.einsum('bhnid,bhnjd->bhnij', q, k) * decay
    q_dec = q * jnp.exp(G)[..., None]
    k_dec = k * jnp.exp(G[..., -1:] - G)[..., None]
    g_last = jnp.exp(G[..., -1])

    def step(s, xs):
        qk_c, qd_c, kd_c, u_c, w_c, gl_c = xs
        v_new = u_c - jnp.einsum('bhid,bhde->bhie', w_c, s)
        o = jnp.einsum('bhid,bhde->bhie', qd_c, s) + jnp.einsum('bhij,bhje->bhie', qk_c, v_new)
        s = s * gl_c[..., None, None] + jnp.einsum('bhid,bhie->bhde', kd_c, v_new)
        return s, o

    xs = tuple(jnp.moveaxis(a, 2, 0) for a in (qk, q_dec, k_dec, u, w, g_last))
    s_fin, o = lax.scan(step, s0.astype(jnp.float32), xs)
    o = jnp.transpose(o, (1, 0, 3, 2, 4)).reshape(b, n * c, nh, dv)[:, :t]
    return o, s_fin


def mla_attend_prompt(q_lat, q_rope, ckv, krope):
    b, s = ckv.shape[:2]
    nb = s // Q_BLOCK
    scale = (MLA_NOPE + MLA_ROPE) ** -0.5
    ql = jnp.swapaxes(q_lat.reshape(b, nb, Q_BLOCK, MLA_H, MLA_KV_LORA), 0, 1)
    qr = jnp.swapaxes(q_rope.reshape(b, nb, Q_BLOCK, MLA_H, MLA_ROPE), 0, 1)
    kpos = jnp.arange(s)

    def block(args):
        i, ql_b, qr_b = args
        sc = jnp.einsum('bqhc,bkc->bhqk', ql_b, ckv) + jnp.einsum('bqhr,bkr->bhqk', qr_b, krope)
        qpos = i * Q_BLOCK + jnp.arange(Q_BLOCK)
        sc = jnp.where(kpos[None, :] <= qpos[:, None], sc.astype(jnp.float32) * scale, -jnp.inf)
        p = jax.nn.softmax(sc, axis=-1).astype(ckv.dtype)
        return jnp.einsum('bhqk,bkc->bqhc', p, ckv)

    o = lax.map(block, (jnp.arange(nb), ql, qr))
    return jnp.swapaxes(o, 0, 1).reshape(b, s, MLA_H, MLA_KV_LORA)


def mla_attend_sample(q_lat, q_rope, ckv_new, krope_new, ckv_past, krope_past):
    t = ckv_new.shape[1]
    p_len = ckv_past.shape[1]
    scale = (MLA_NOPE + MLA_ROPE) ** -0.5
    s_past = jnp.einsum('bqhc,bkc->bhqk', q_lat, ckv_past) + jnp.einsum('bqhr,bkr->bhqk', q_rope, krope_past)
    s_new = jnp.einsum('bqhc,bkc->bhqk', q_lat, ckv_new) + jnp.einsum('bqhr,bkr->bhqk', q_rope, krope_new)
    causal = jnp.tril(jnp.ones((t, t), dtype=bool))
    s_new = jnp.where(causal, s_new.astype(jnp.float32) * scale, -jnp.inf)
    sc = jnp.concatenate([s_past.astype(jnp.float32) * scale, s_new], axis=-1)
    p = jax.nn.softmax(sc, axis=-1).astype(ckv_new.dtype)
    return (jnp.einsum('bhqk,bkc->bqhc', p[..., :p_len], ckv_past)
            + jnp.einsum('bhqk,bkc->bqhc', p[..., p_len:], ckv_new))


def gather_pages(pool, page_table):
    g = pool[page_table]
    return g.reshape(page_table.shape[0], -1, pool.shape[-1])


def mixing_sublayer(h, cos, sin, gdn_s0, conv_buf, past_ckv, past_krope, w_in, conv_w, a_log, dt_bias,
                    gdn_norm_g, q_norm_g, w_uq, kv_norm_g, w_uk, w_uv, w_out):
    b, t, _ = h.shape
    sizes = (GDN_CONV_CH, GDN_VW, GDN_H, GDN_H, MLA_Q_LORA, MLA_KV_LORA, MLA_ROPE)
    cuts = [int(c) for c in np.cumsum(sizes)[:-1]]
    qkv, z, ga, gb, cq, ckv, kr = jnp.split(h @ w_in, cuts, axis=-1)
    qkv, conv_new = causal_conv(qkv, conv_buf, conv_w)
    q, k, v = jnp.split(qkv, [GDN_QK, 2 * GDN_QK], axis=-1)
    q = l2norm(q.reshape(b, t, GDN_H, GDN_DK)) * (GDN_DK ** -0.5)
    k = l2norm(k.reshape(b, t, GDN_H, GDN_DK))
    v = v.reshape(b, t, GDN_H, GDN_DV)
    beta = jax.nn.sigmoid(gb.astype(jnp.float32))
    g = -jnp.exp(a_log.astype(jnp.float32)) * jax.nn.softplus(ga.astype(jnp.float32) + dt_bias.astype(jnp.float32))
    o_gdn, s_new = gated_delta_chunked(q, k, v, g, beta, gdn_s0)
    o_gdn = rmsnorm(o_gdn.astype(h.dtype), gdn_norm_g) * jax.nn.silu(z.reshape(b, t, GDN_H, GDN_DV))
    cq = rmsnorm(cq, q_norm_g)
    qh = jnp.einsum('btc,chd->bthd', cq, w_uq)
    q_nope, q_rope = jnp.split(qh, [MLA_NOPE], axis=-1)
    q_rope = apply_rope(q_rope, cos[:, None, :], sin[:, None, :])
    ckv = rmsnorm(ckv, kv_norm_g)
    kr = apply_rope(kr, cos, sin)
    q_lat = jnp.einsum('bthd,chd->bthc', q_nope, w_uk)
    if past_ckv is None:
        o_lat = mla_attend_prompt(q_lat, q_rope, ckv, kr)
    else:
        o_lat = mla_attend_sample(q_lat, q_rope, ckv, kr, past_ckv, past_krope)
    o_mla = jnp.einsum('bthc,chv->bthv', o_lat, w_uv)
    mixed = jnp.concatenate([o_gdn.reshape(b, t, GDN_VW), o_mla.reshape(b, t, MLA_H * MLA_V)], axis=-1)
    return mixed @ w_out, s_new, conv_new, ckv, kr


def mem_kv(mem, mem_norm_g, wk, wv):
    b = mem.shape[0]
    m = rmsnorm(mem, mem_norm_g)
    return (m @ wk).reshape(b, N_MEM, MEM_H, MEM_DH), (m @ wv).reshape(b, N_MEM, MEM_H, MEM_DH)


def mem_attend(h, mk, mv, wq, wo):
    b, t, _ = h.shape
    q = (h @ wq).reshape(b, t, MEM_H, MEM_DH)
    s = jnp.einsum('bthd,bmhd->bhtm', q, mk).astype(jnp.float32) * (MEM_DH ** -0.5)
    p = jax.nn.softmax(s, axis=-1).astype(h.dtype)
    o = jnp.einsum('bhtm,bmhd->bthd', p, mv).reshape(b, t, MEM_H * MEM_DH)
    return o @ wo


def peer(h, wq, subkeys, u_tab, v_tab):
    b, t, d = h.shape
    n = b * t
    blk = PEER_BLOCK if n >= PEER_BLOCK else n
    pad = (-n) % blk
    xb = jnp.pad(h.reshape(n, d), ((0, pad), (0, 0))).reshape(-1, blk, d)

    def one_block(x):
        q = jnp.einsum('nd,dhq->nhq', x, wq)
        q1, q2 = jnp.split(q, 2, axis=-1)
        s1 = jnp.einsum('nhq,hkq->nhk', q1, subkeys[:, 0])
        s2 = jnp.einsum('nhq,hkq->nhk', q2, subkeys[:, 1])
        v1, i1 = lax.top_k(s1, PEER_TOPK)
        v2, i2 = lax.top_k(s2, PEER_TOPK)
        cand = (v1[..., :, None] + v2[..., None, :]).reshape(blk, PEER_H, PEER_TOPK * PEER_TOPK)
        cidx = (i1[..., :, None] * PEER_N_KEYS + i2[..., None, :]).reshape(blk, PEER_H, PEER_TOPK * PEER_TOPK)
        sc, j = lax.top_k(cand, PEER_TOPK)
        eid = jnp.take_along_axis(cidx, j, axis=-1)
        gate = jax.nn.softmax(sc.astype(jnp.float32), axis=-1).astype(x.dtype)
        act = jax.nn.gelu(jnp.einsum('nhkd,nd->nhk', u_tab[eid], x))
        return jnp.einsum('nhk,nhkd->nd', gate * act, v_tab[eid])

    y = lax.map(one_block, xb).reshape(-1, d)[:n]
    return y.reshape(b, t, d)


def setup_inputs(seed: int = 0) -> dict:
    key = jax.random.key(seed)
    ks = iter(jax.random.split(key, 48))

    def nrm(shape, scale):
        return jax.random.normal(next(ks), shape, jnp.float32) * scale

    def gain(shape):
        return 1.0 + nrm(shape, 0.02)

    n_pages = PAST_LEN // PAGE_SIZE
    n_used = DEC_BATCH * n_pages
    n_pool = n_used + n_used // 4
    page_table = jax.random.permutation(next(ks), n_pool)[:n_used].reshape(DEC_BATCH, n_pages).astype(jnp.int32)
    return {
        'x_prompt': nrm((BATCH, SEQ, D_MODEL), 1.0),
        'x_sample': nrm((DEC_BATCH, DEC_SEQ, D_MODEL), 1.0),
        'mem_prompt': nrm((BATCH, N_MEM, D_MODEL), 1.0),
        'cache_mla_ckv': nrm((DEPTH, n_pool, PAGE_SIZE, MLA_KV_LORA), 1.0),
        'cache_mla_krope': nrm((DEPTH, n_pool, PAGE_SIZE, MLA_ROPE), 1.0),
        'cache_mem_k': nrm((DEPTH, DEC_BATCH, N_MEM, MEM_H, MEM_DH), 1.0),
        'cache_mem_v': nrm((DEPTH, DEC_BATCH, N_MEM, MEM_H, MEM_DH), 1.0),
        'state_gdn': nrm((DEPTH, DEC_BATCH, GDN_H, GDN_DK, GDN_DV), 0.5),
        'state_gdn_conv': nrm((DEPTH, DEC_BATCH, GDN_CONV - 1, GDN_CONV_CH), 1.0),
        'page_table': page_table,
        'ln_mix_g': gain((DEPTH, D_MODEL)),
        'w_in': nrm((DEPTH, D_MODEL, D_IN), D_MODEL ** -0.5),
        'gdn_conv_w': nrm((DEPTH, GDN_CONV, GDN_CONV_CH), 0.5),
        'gdn_a_log': jnp.log(jax.random.uniform(next(ks), (DEPTH, GDN_H), jnp.float32, 1.0, 16.0)),
        'gdn_dt_bias': jnp.log(jnp.expm1(jax.random.uniform(next(ks), (DEPTH, GDN_H), jnp.float32, 0.001, 0.1))),
        'gdn_norm_g': gain((DEPTH, GDN_DV)),
        'mla_q_norm_g': gain((DEPTH, MLA_Q_LORA)),
        'mla_w_uq': nrm((DEPTH, MLA_Q_LORA, MLA_H, MLA_NOPE + MLA_ROPE), MLA_Q_LORA ** -0.5),
        'mla_kv_norm_g': gain((DEPTH, MLA_KV_LORA)),
        'mla_w_uk': nrm((DEPTH, MLA_KV_LORA, MLA_H, MLA_NOPE), MLA_KV_LORA ** -0.5),
        'mla_w_uv': nrm((DEPTH, MLA_KV_LORA, MLA_H, MLA_V), MLA_KV_LORA ** -0.5),
        'w_out': nrm((DEPTH, D_MIX, D_MODEL), D_MIX ** -0.5),
        'ln_mem_g': gain((DEPTH, D_MODEL)),
        'mem_norm_g': gain((DEPTH, D_MODEL)),
        'mem_wq': nrm((DEPTH, D_MODEL, MEM_H * MEM_DH), D_MODEL ** -0.5),
        'mem_wk': nrm((DEPTH, D_MODEL, MEM_H * MEM_DH), D_MODEL ** -0.5),
        'mem_wv': nrm((DEPTH, D_MODEL, MEM_H * MEM_DH), D_MODEL ** -0.5),
        'mem_wo': nrm((DEPTH, MEM_H * MEM_DH, D_MODEL), (MEM_H * MEM_DH) ** -0.5),
        'ln_peer_g': gain((DEPTH, D_MODEL)),
        'peer_wq': nrm((DEPTH, D_MODEL, PEER_H, PEER_DQ), D_MODEL ** -0.5),
        'peer_subkeys': nrm((DEPTH, PEER_H, 2, PEER_N_KEYS, PEER_DQ // 2), (PEER_DQ // 2) ** -0.5),
        'peer_u': nrm((DEPTH, PEER_N_EXPERTS, D_MODEL), D_MODEL ** -0.5),
        'peer_v': nrm((DEPTH, PEER_N_EXPERTS, D_MODEL), 0.1),
        'final_norm_g': gain((D_MODEL,)),
    }


def reference(x_prompt, x_sample, mem_prompt, cache_mla_ckv, cache_mla_krope, cache_mem_k, cache_mem_v,
              state_gdn, state_gdn_conv, page_table, ln_mix_g, w_in, gdn_conv_w, gdn_a_log, gdn_dt_bias,
              gdn_norm_g, mla_q_norm_g, mla_w_uq, mla_kv_norm_g, mla_w_uk, mla_w_uv, w_out, ln_mem_g,
              mem_norm_g, mem_wq, mem_wk, mem_wv, mem_wo, ln_peer_g, peer_wq, peer_subkeys, peer_u, peer_v,
              final_norm_g):
    b_p, s_len, _ = x_prompt.shape
    t_new = x_sample.shape[1]
    past_len = page_table.shape[1] * PAGE_SIZE
    cos_p, sin_p = rope_tables(jnp.arange(s_len))
    cos_s, sin_s = rope_tables(past_len + jnp.arange(t_new))
    xp, xs = x_prompt, x_sample
    p_ckv, p_kr, p_gdn, p_conv, p_mk, p_mv = [], [], [], [], [], []
    s_ckv, s_kr, s_gdn, s_conv = [], [], [], []
    for l in range(DEPTH):
        mix_w = (w_in[l], gdn_conv_w[l], gdn_a_log[l], gdn_dt_bias[l], gdn_norm_g[l], mla_q_norm_g[l],
                 mla_w_uq[l], mla_kv_norm_g[l], mla_w_uk[l], mla_w_uv[l], w_out[l])
        peer_w = (peer_wq[l], peer_subkeys[l], peer_u[l], peer_v[l])
        s0 = jnp.zeros((b_p, GDN_H, GDN_DK, GDN_DV), jnp.float32)
        c0 = jnp.zeros((b_p, GDN_CONV - 1, GDN_CONV_CH), xp.dtype)
        o, sp, cp, ckvp, krp = mixing_sublayer(rmsnorm(xp, ln_mix_g[l]), cos_p, sin_p, s0, c0, None, None, *mix_w)
        xp = xp + o
        mk_p, mv_p = mem_kv(mem_prompt, mem_norm_g[l], mem_wk[l], mem_wv[l])
        xp = xp + mem_attend(rmsnorm(xp, ln_mem_g[l]), mk_p, mv_p, mem_wq[l], mem_wo[l])
        xp = xp + peer(rmsnorm(xp, ln_peer_g[l]), *peer_w)
        p_ckv.append(ckvp)
        p_kr.append(krp)
        p_gdn.append(sp.astype(state_gdn.dtype))
        p_conv.append(cp)
        p_mk.append(mk_p)
        p_mv.append(mv_p)
        past_ckv = gather_pages(cache_mla_ckv[l], page_table)
        past_kr = gather_pages(cache_mla_krope[l], page_table)
        o, ss, cs, ckvs, krs = mixing_sublayer(rmsnorm(xs, ln_mix_g[l]), cos_s, sin_s, state_gdn[l],
                                               state_gdn_conv[l], past_ckv, past_kr, *mix_w)
        xs = xs + o
        xs = xs + mem_attend(rmsnorm(xs, ln_mem_g[l]), cache_mem_k[l], cache_mem_v[l], mem_wq[l], mem_wo[l])
        xs = xs + peer(rmsnorm(xs, ln_peer_g[l]), *peer_w)
        s_ckv.append(ckvs)
        s_kr.append(krs)
        s_gdn.append(ss.astype(state_gdn.dtype))
        s_conv.append(cs)
    y_prompt = rmsnorm(xp, final_norm_g)
    y_sample = rmsnorm(xs, final_norm_g)
    return (y_prompt, y_sample, jnp.stack(p_ckv), jnp.stack(p_kr), jnp.stack(p_gdn), jnp.stack(p_conv),
            jnp.stack(p_mk), jnp.stack(p_mv), jnp.stack(s_ckv), jnp.stack(s_kr), jnp.stack(s_gdn),
            jnp.stack(s_conv))
```

```python
import functools
import math

import jax
import jax.numpy as jnp
from jax import lax
from jax.experimental import pallas as pl
from jax.experimental.pallas import tpu as pltpu

F32 = jnp.float32
BF16 = jnp.bfloat16
HIGHEST = lax.Precision.HIGHEST

D_MODEL = 1024
GDN_H = 4
GDN_DK = 128
GDN_CONV = 4
GDN_QKV = 1536
MLA_H = 4
MLA_Q_LORA = 384
MLA_KV_LORA = 256
MLA_NOPE = 128
MLA_ROPE = 64
ROPE_THETA = 10000.0
PAGE_SIZE = 128
N_MEM = 256
MEM_H = 4
MEM_DH = 128
PEER_H = 8
PEER_N_KEYS = 128
PEER_TOPK = 16
EPS = 1e-6

LANES = 128
SUBLANES = 8
CHUNK = 128
MLA_CAT = MLA_KV_LORA + LANES
D_PROJ = 2816
COL_G5 = 5
COL_QKV = 6
COL_Z = 18
VMEM_LIMIT = 56 * 1024 * 1024

NT_DIMS = (((1,), (1,)), ((), ()))


def _dot(a, b):
    return jnp.dot(a.astype(BF16), b.astype(BF16), preferred_element_type=F32)


def _dot_nt(a, b):
    return lax.dot_general(a.astype(BF16), b.astype(BF16), NT_DIMS, preferred_element_type=F32)


def _sigmoid(x):
    return 1.0 / (1.0 + jnp.exp(-x))


def _rms(x, g):
    return x * lax.rsqrt(jnp.mean(x * x, axis=-1, keepdims=True) + EPS) * g


def _params(*sem):
    return pltpu.CompilerParams(dimension_semantics=sem, vmem_limit_bytes=VMEM_LIMIT)


def _norm_matmul_kernel(x_ref, g_ref, w_ref, o_ref, *h_ref):
    y = _rms(x_ref[...], g_ref[...])
    if h_ref:
        h_ref[0][...] = y
    o_ref[...] = _dot(y, w_ref[...])


def norm_matmul(x, g, w, *, emit_normed=False, tm=256):
    n, d = x.shape
    dout = w.shape[1]
    out_shape = [jax.ShapeDtypeStruct((n, dout), F32)]
    out_specs = [pl.BlockSpec((tm, dout), lambda i: (i, 0))]
    if emit_normed:
        out_shape.append(jax.ShapeDtypeStruct((n, d), F32))
        out_specs.append(pl.BlockSpec((tm, d), lambda i: (i, 0)))
    res = pl.pallas_call(
        _norm_matmul_kernel,
        grid=(n // tm,),
        in_specs=[pl.BlockSpec((tm, d), lambda i: (i, 0)),
                  pl.BlockSpec((1, d), lambda i: (0, 0)),
                  pl.BlockSpec((d, dout), lambda i: (0, 0))],
        out_specs=out_specs,
        out_shape=out_shape,
        compiler_params=_params("arbitrary"),
        name="norm_matmul",
    )(x, g.reshape(1, d), w)
    return res if emit_normed else res[0]


def _final_norm_kernel(x_ref, g_ref, o_ref):
    o_ref[...] = _rms(x_ref[...], g_ref[...])


def final_norm(x, g, tm=256):
    n, d = x.shape
    return pl.pallas_call(
        _final_norm_kernel,
        grid=(n // tm,),
        in_specs=[pl.BlockSpec((tm, d), lambda i: (i, 0)), pl.BlockSpec((1, d), lambda i: (0, 0))],
        out_specs=pl.BlockSpec((tm, d), lambda i: (i, 0)),
        out_shape=jax.ShapeDtypeStruct((n, d), F32),
        compiler_params=_params("arbitrary"),
        name="final_norm",
    )(x, g.reshape(1, d))


def _gdn_kernel(alog_ref, dtb_ref, q_ref, k_ref, v_ref, z_ref, g5_ref, cq_ref, ck_ref, cv_ref,
                wq_ref, wk_ref, wv_ref, ng_ref, s0_ref, o_ref, sout_ref,
                s_scr, carq, cark, carv, *, rb, valid, nt):
    h = pl.program_id(1)
    t = pl.program_id(2)

    @pl.when(t == 0)
    def _():
        carq[...] = cq_ref[0]
        cark[...] = ck_ref[0]
        carv[...] = cv_ref[0]
        s_scr[...] = s0_ref[0, 0]

    def conv(x_ref, car, w_ref):
        x = x_ref[...]
        cat = jnp.concatenate([car[...], x], axis=0)
        w = w_ref[...]
        y = cat[SUBLANES:SUBLANES + rb] * w[3:4]
        for i in range(1, GDN_CONV):
            y = y + cat[SUBLANES - i:SUBLANES - i + rb] * w[3 - i:4 - i]
        car[...] = x[rb - SUBLANES:rb]
        return y * _sigmoid(y)

    def l2n(x):
        return x * lax.rsqrt(jnp.sum(x * x, axis=-1, keepdims=True) + EPS)

    q = l2n(conv(q_ref, carq, wq_ref)) * (GDN_DK ** -0.5)
    k = l2n(conv(k_ref, cark, wk_ref))
    v = conv(v_ref, carv, wv_ref)

    g5 = g5_ref[...]
    lane = lax.broadcasted_iota(jnp.int32, g5.shape, 1)
    ga = jnp.sum(jnp.where(lane == MLA_ROPE + h, g5, 0.0), axis=1, keepdims=True)
    gb = jnp.sum(jnp.where(lane == MLA_ROPE + GDN_H + h, g5, 0.0), axis=1, keepdims=True)
    beta = _sigmoid(gb)
    sp = ga + dtb_ref[h]
    softplus = jnp.maximum(sp, 0.0) + jnp.log(1.0 + jnp.exp(-jnp.abs(sp)))
    g = -jnp.exp(jnp.full((rb, 1), alog_ref[h], F32)) * softplus
    if valid < rb:
        rvalid = lax.broadcasted_iota(jnp.int32, (rb, 1), 0) < valid
        beta = jnp.where(rvalid, beta, 0.0)
        g = jnp.where(rvalid, g, 0.0)

    if rb < CHUNK:
        def padr(a):
            return jnp.concatenate([a, jnp.zeros((CHUNK - rb, a.shape[1]), F32)], axis=0)
        q, k, v, beta, g = padr(q), padr(k), padr(v), padr(beta), padr(g)

    row = lax.broadcasted_iota(jnp.int32, (CHUNK, CHUNK), 0)
    col = lax.broadcasted_iota(jnp.int32, (CHUNK, CHUNK), 1)
    causal = row >= col
    strict = row > col
    eye = (row == col).astype(F32)
    g_b = jnp.broadcast_to(g, (CHUNK, CHUNK))
    gc = jnp.dot(causal.astype(F32), g_b, precision=HIGHEST, preferred_element_type=F32)
    gr = gc.T
    decay = jnp.where(causal, jnp.exp(jnp.where(causal, gc - gr, 0.0)), 0.0)
    kb = k * beta
    low = jnp.where(strict, _dot_nt(kb, k) * decay, 0.0)
    m = -low
    tinv = eye + m
    for _ in range(int(math.log2(CHUNK)) - 1):
        m = jnp.dot(m, m, precision=HIGHEST, preferred_element_type=F32)
        tinv = tinv + jnp.dot(tinv, m, precision=HIGHEST, preferred_element_type=F32)
    eg = jnp.exp(gc)
    u = _dot(tinv, v * beta)
    w = _dot(tinv, kb * eg)
    qk = _dot_nt(q, k) * decay
    g_last = gc[CHUNK - 1:CHUNK, :]
    s = s_scr[...]
    v_new = u - _dot(w, s)
    o = _dot(q * eg, s) + _dot(qk, v_new)
    k_dec = k * jnp.exp(g_last - gc)
    s_new = s * jnp.exp(g_last) + _dot(k_dec.T, v_new)
    s_scr[...] = s_new

    o = o[:rb]
    zz = z_ref[...]
    o_ref[...] = _rms(o, ng_ref[...]) * (zz * _sigmoid(zz))

    @pl.when(t == nt - 1)
    def _():
        sout_ref[0, 0] = s_new


def gdn(proj, cbuf, s0, conv_w, a_log, dt_bias, norm_g, *, batch, rb, valid):
    rows = proj.shape[0]
    nt = rows // (batch * rb)
    kern = functools.partial(_gdn_kernel, rb=rb, valid=valid, nt=nt)

    def colblk(c0):
        return pl.BlockSpec((rb, LANES), lambda b, h, t, *_: (b * nt + t, c0 + h))

    def cblk(c0):
        return pl.BlockSpec((1, SUBLANES, LANES), lambda b, h, t, *_: (b, 0, c0 + h))

    def wblk(c0):
        return pl.BlockSpec((GDN_CONV, LANES), lambda b, h, t, *_: (0, c0 + h))

    grid_spec = pltpu.PrefetchScalarGridSpec(
        num_scalar_prefetch=2,
        grid=(batch, GDN_H, nt),
        in_specs=[colblk(COL_QKV), colblk(COL_QKV + 4), colblk(COL_QKV + 8), colblk(COL_Z),
                  pl.BlockSpec((rb, LANES), lambda b, h, t, *_: (b * nt + t, COL_G5)),
                  cblk(0), cblk(4), cblk(8), wblk(0), wblk(4), wblk(8),
                  pl.BlockSpec((1, LANES), lambda b, h, t, *_: (0, 0)),
                  pl.BlockSpec((1, 1, GDN_DK, LANES), lambda b, h, t, *_: (b, h, 0, 0))],
        out_specs=[pl.BlockSpec((rb, LANES), lambda b, h, t, *_: (b * nt + t, h)),
                   pl.BlockSpec((1, 1, GDN_DK, LANES), lambda b, h, t, *_: (b, h, 0, 0))],
        scratch_shapes=[pltpu.VMEM((GDN_DK, LANES), F32)] + [pltpu.VMEM((SUBLANES, LANES), F32)] * 3,
    )
    return pl.pallas_call(
        kern, grid_spec=grid_spec,
        out_shape=[jax.ShapeDtypeStruct((rows, GDN_H * LANES), F32),
                   jax.ShapeDtypeStruct((batch, GDN_H, GDN_DK, LANES), F32)],
        compiler_params=_params("arbitrary", "arbitrary", "arbitrary"),
        name="gdn",
    )(a_log, dt_bias, proj, proj, proj, proj, proj, cbuf, cbuf, cbuf, conv_w, conv_w, conv_w,
      norm_g.reshape(1, LANES), s0)


def _rope(x, cos, sin_signed):
    lane = lax.broadcasted_iota(jnp.int32, x.shape, 1)
    first_half = (lane % MLA_ROPE) < (MLA_ROPE // 2)
    rot = jnp.where(first_half, pltpu.roll(x, LANES - MLA_ROPE // 2, 1), pltpu.roll(x, MLA_ROPE // 2, 1))
    return x * cos + rot * sin_signed


def _mla_prep_kernel(p_ref, cos_ref, sin_ref, qg_ref, kvg_ref, wuq_ref, wuk_ref,
                     qcat_ref, kvcat_ref, ckv_ref, kr_ref):
    blk = p_ref[...]
    cos = cos_ref[...]
    sin = sin_ref[...]
    cq = _rms(blk[:, :MLA_Q_LORA], qg_ref[...])
    ckv = _rms(blk[:, MLA_Q_LORA:MLA_Q_LORA + MLA_KV_LORA], kvg_ref[...])
    kr = _rope(blk[:, MLA_Q_LORA + MLA_KV_LORA:], cos, sin)
    ckv_ref[...] = ckv
    kr_ref[...] = kr[:, :MLA_ROPE]
    kvcat_ref[:, :MLA_KV_LORA] = ckv.astype(BF16)
    kvcat_ref[:, MLA_KV_LORA:] = kr.astype(BF16)
    qh = _dot(cq, wuq_ref[...])
    scale = (MLA_NOPE + MLA_ROPE) ** -0.5
    for h in range(MLA_H):
        q_lat = _dot(qh[:, h * LANES:(h + 1) * LANES], wuk_ref[h])
        q_rope = _rope(qh[:, (MLA_H + h) * LANES:(MLA_H + h + 1) * LANES], cos, sin)
        qcat_ref[:, h * MLA_CAT:h * MLA_CAT + MLA_KV_LORA] = (q_lat * scale).astype(BF16)
        qcat_ref[:, h * MLA_CAT + MLA_KV_LORA:(h + 1) * MLA_CAT] = (q_rope * scale).astype(BF16)


def mla_prep(proj, cos, sin, qg, kvg, wuq, wuk, tm=256):
    n = proj.shape[0]
    wmla = MLA_Q_LORA + MLA_KV_LORA + LANES
    return pl.pallas_call(
        _mla_prep_kernel,
        grid=(n // tm,),
        in_specs=[pl.BlockSpec((tm, wmla), lambda i: (i, 0)),
                  pl.BlockSpec((tm, LANES), lambda i: (i, 0)),
                  pl.BlockSpec((tm, LANES), lambda i: (i, 0)),
                  pl.BlockSpec((1, MLA_Q_LORA), lambda i: (0, 0)),
                  pl.BlockSpec((1, MLA_KV_LORA), lambda i: (0, 0)),
                  pl.BlockSpec(wuq.shape, lambda i: (0, 0)),
                  pl.BlockSpec(wuk.shape, lambda i: (0, 0, 0))],
        out_specs=[pl.BlockSpec((tm, MLA_H * MLA_CAT), lambda i: (i, 0)),
                   pl.BlockSpec((tm, MLA_CAT), lambda i: (i, 0)),
                   pl.BlockSpec((tm, MLA_KV_LORA), lambda i: (i, 0)),
                   pl.BlockSpec((tm, MLA_ROPE), lambda i: (i, 0))],
        out_shape=[jax.ShapeDtypeStruct((n, MLA_H * MLA_CAT), BF16),
                   jax.ShapeDtypeStruct((n, MLA_CAT), BF16),
                   jax.ShapeDtypeStruct((n, MLA_KV_LORA), F32),
                   jax.ShapeDtypeStruct((n, MLA_ROPE), F32)],
        compiler_params=_params("arbitrary"),
        name="mla_prep",
    )(proj, cos, sin, qg.reshape(1, -1), kvg.reshape(1, -1), wuq, wuk)


def _flash_kernel(q_ref, kv_ref, o_ref, m_scr, l_scr, acc_scr, *, tq, tk, nk):
    qi = pl.program_id(1)
    kj = pl.program_id(2)

    @pl.when(kj == 0)
    def _():
        m_scr[...] = jnp.full(m_scr.shape, -jnp.inf, F32)
        l_scr[...] = jnp.zeros(l_scr.shape, F32)
        acc_scr[...] = jnp.zeros(acc_scr.shape, F32)

    @pl.when(kj * tk <= qi * tq + tq - 1)
    def _():
        kv = kv_ref[...]
        qpos = qi * tq + lax.broadcasted_iota(jnp.int32, (tq, tk), 0)
        kpos = kj * tk + lax.broadcasted_iota(jnp.int32, (tq, tk), 1)
        mask = kpos <= qpos
        for h in range(MLA_H):
            s = lax.dot_general(q_ref[:, h * MLA_CAT:(h + 1) * MLA_CAT], kv, NT_DIMS,
                                preferred_element_type=F32)
            s = jnp.where(mask, s, -jnp.inf)
            m_prev = m_scr[h]
            m_new = jnp.maximum(m_prev, jnp.max(s, axis=1, keepdims=True))
            alpha = jnp.exp(m_prev - m_new)
            p = jnp.exp(s - m_new)
            l_scr[h] = alpha * l_scr[h] + jnp.sum(p, axis=1, keepdims=True)
            acc_scr[h] = alpha * acc_scr[h] + jnp.dot(p.astype(BF16), kv[:, :MLA_KV_LORA],
                                                      preferred_element_type=F32)
            m_scr[h] = m_new

    @pl.when(kj == nk - 1)
    def _():
        for h in range(MLA_H):
            o_ref[:, h * MLA_KV_LORA:(h + 1) * MLA_KV_LORA] = acc_scr[h] / l_scr[h]


def mla_flash(qcat, kvcat, *, batch, seq, tq=256, tk=256):
    nq, nk = seq // tq, seq // tk
    kern = functools.partial(_flash_kernel, tq=tq, tk=tk, nk=nk)

    def kv_map(b, i, j):
        last = (i * tq + tq - 1) // tk
        return (b * nk + jnp.minimum(j, last), 0)

    return pl.pallas_call(
        kern,
        grid=(batch, nq, nk),
        in_specs=[pl.BlockSpec((tq, MLA_H * MLA_CAT), lambda b, i, j: (b * nq + i, 0)),
                  pl.BlockSpec((tk, MLA_CAT), kv_map)],
        out_specs=pl.BlockSpec((tq, MLA_H * MLA_KV_LORA), lambda b, i, j: (b * nq + i, 0)),
        out_shape=jax.ShapeDtypeStruct((batch * seq, MLA_H * MLA_KV_LORA), F32),
        scratch_shapes=[pltpu.VMEM((MLA_H, tq, 1), F32), pltpu.VMEM((MLA_H, tq, 1), F32),
                        pltpu.VMEM((MLA_H, tq, MLA_KV_LORA), F32)],
        compiler_params=_params("arbitrary", "arbitrary", "arbitrary"),
        name="mla_flash",
    )(qcat, kvcat)


PAGES_PER_STEP = 8


def _mla_paged_kernel(pt_ref, q_ref, kvn_ref, *refs, t_new, n_steps):
    ckv_refs = refs[:PAGES_PER_STEP]
    kr_refs = refs[PAGES_PER_STEP:2 * PAGES_PER_STEP]
    o_ref, m_scr, l_scr, acc_scr = refs[2 * PAGES_PER_STEP:]
    j = pl.program_id(1)

    @pl.when(j == 0)
    def _():
        m_scr[...] = jnp.full(m_scr.shape, -jnp.inf, F32)
        l_scr[...] = jnp.zeros(l_scr.shape, F32)
        acc_scr[...] = jnp.zeros(acc_scr.shape, F32)

    q = q_ref[0]
    q_lat = q[:, :MLA_KV_LORA]
    q_rope = q[:, MLA_KV_LORA:MLA_KV_LORA + MLA_ROPE]

    def update(s_list, v_list):
        m_prev = m_scr[...]
        m_new = m_prev
        for s in s_list:
            m_new = jnp.maximum(m_new, jnp.max(s, axis=1, keepdims=True))
        alpha = jnp.exp(m_prev - m_new)
        l_new = alpha * l_scr[...]
        acc = alpha * acc_scr[...]
        for s, vv in zip(s_list, v_list):
            p = jnp.exp(s - m_new)
            l_new = l_new + jnp.sum(p, axis=1, keepdims=True)
            acc = acc + jnp.dot(p.astype(BF16), vv, preferred_element_type=F32)
        m_scr[...] = m_new
        l_scr[...] = l_new
        acc_scr[...] = acc

    s_list, v_list = [], []
    for i in range(PAGES_PER_STEP):
        ck = ckv_refs[i][0, 0].astype(BF16)
        kr = kr_refs[i][0, 0].astype(BF16)
        s = (lax.dot_general(q_lat, ck, NT_DIMS, preferred_element_type=F32)
             + lax.dot_general(q_rope, kr, NT_DIMS, preferred_element_type=F32))
        s_list.append(s)
        v_list.append(ck)
    update(s_list, v_list)

    @pl.when(j == n_steps - 1)
    def _():
        kvn = kvn_ref[0]
        kvn = jnp.concatenate([kvn, jnp.zeros((LANES - kvn.shape[0], MLA_CAT), BF16)], axis=0)
        s = lax.dot_general(q, kvn, NT_DIMS, preferred_element_type=F32)
        nrow = q.shape[0]
        tok = lax.broadcasted_iota(jnp.int32, (nrow, LANES), 0) // MLA_H
        cpos = lax.broadcasted_iota(jnp.int32, (nrow, LANES), 1)
        s = jnp.where((cpos <= tok) & (cpos < t_new), s, -jnp.inf)
        update([s], [kvn[:, :MLA_KV_LORA]])
        o_ref[0] = acc_scr[...] / l_scr[...]


def mla_paged(qcat, kvcat_new, cache_ckv, cache_kr, page_table, *, layer, t_new):
    nb, n_pages = page_table.shape
    n_steps = n_pages // PAGES_PER_STEP
    nrow = t_new * MLA_H
    q3 = qcat.reshape(nb, nrow, MLA_CAT)
    kvn = jnp.pad(kvcat_new.reshape(nb, t_new, MLA_CAT), ((0, 0), (0, 2 * SUBLANES - t_new), (0, 0)))
    kern = functools.partial(_mla_paged_kernel, t_new=t_new, n_steps=n_steps)

    def page_spec(width, i):
        return pl.BlockSpec((1, 1, PAGE_SIZE, width),
                            lambda b, j, pt: (layer, pt[b, j * PAGES_PER_STEP + i], 0, 0))

    grid_spec = pltpu.PrefetchScalarGridSpec(
        num_scalar_prefetch=1,
        grid=(nb, n_steps),
        in_specs=[pl.BlockSpec((1, nrow, MLA_CAT), lambda b, j, pt: (b, 0, 0)),
                  pl.BlockSpec((1, 2 * SUBLANES, MLA_CAT), lambda b, j, pt: (b, 0, 0))]
                 + [page_spec(MLA_KV_LORA, i) for i in range(PAGES_PER_STEP)]
                 + [page_spec(MLA_ROPE, i) for i in range(PAGES_PER_STEP)],
        out_specs=pl.BlockSpec((1, nrow, MLA_KV_LORA), lambda b, j, pt: (b, 0, 0)),
        scratch_shapes=[pltpu.VMEM((nrow, 1), F32), pltpu.VMEM((nrow, 1), F32),
                        pltpu.VMEM((nrow, MLA_KV_LORA), F32)],
    )
    out = pl.pallas_call(
        kern, grid_spec=grid_spec,
        out_shape=jax.ShapeDtypeStruct((nb, nrow, MLA_KV_LORA), F32),
        compiler_params=_params("arbitrary", "arbitrary"),
        name="mla_paged",
    )(page_table, q3, kvn, *([cache_ckv] * PAGES_PER_STEP), *([cache_kr] * PAGES_PER_STEP))
    return out.reshape(nb * t_new, MLA_H * MLA_KV_LORA)


def _outproj_kernel(x_ref, og_ref, ol_ref, wuv_ref, wo_ref, o_ref):
    parts = [og_ref[...]]
    for h in range(MLA_H):
        parts.append(_dot(ol_ref[:, h * MLA_KV_LORA:(h + 1) * MLA_KV_LORA], wuv_ref[h]))
    mixed = jnp.concatenate(parts, axis=1)
    o_ref[...] = x_ref[...] + _dot(mixed, wo_ref[...])


def outproj(x, og, olat, wuv, wout, tm=256):
    n = x.shape[0]
    return pl.pallas_call(
        _outproj_kernel,
        grid=(n // tm,),
        in_specs=[pl.BlockSpec((tm, D_MODEL), lambda i: (i, 0)),
                  pl.BlockSpec((tm, og.shape[1]), lambda i: (i, 0)),
                  pl.BlockSpec((tm, olat.shape[1]), lambda i: (i, 0)),
                  pl.BlockSpec(wuv.shape, lambda i: (0, 0, 0)),
                  pl.BlockSpec(wout.shape, lambda i: (0, 0))],
        out_specs=pl.BlockSpec((tm, D_MODEL), lambda i: (i, 0)),
        out_shape=jax.ShapeDtypeStruct((n, D_MODEL), F32),
        compiler_params=_params("arbitrary"),
        name="outproj",
    )(x, og, olat, wuv, wout)


def _mem_attn_kernel(x_ref, q_ref, mk_ref, mv_ref, wo_ref, o_ref):
    parts = []
    scale = MEM_DH ** -0.5
    for h in range(MEM_H):
        sl = slice(h * MEM_DH, (h + 1) * MEM_DH)
        s = _dot_nt(q_ref[:, sl], mk_ref[:, sl]) * scale
        s = s - jnp.max(s, axis=1, keepdims=True)
        p = jnp.exp(s)
        p = p / jnp.sum(p, axis=1, keepdims=True)
        parts.append(_dot(p, mv_ref[:, sl]))
    o_ref[...] = x_ref[...] + _dot(jnp.concatenate(parts, axis=1), wo_ref[...])


def mem_attn(x, q, mk, mv, wo, *, batch, tq, kblk0, vblk0, vcol):
    n = x.shape[0]
    nq = n // (batch * tq)
    dh = MEM_H * MEM_DH
    return pl.pallas_call(
        _mem_attn_kernel,
        grid=(batch, nq),
        in_specs=[pl.BlockSpec((tq, D_MODEL), lambda b, i: (b * nq + i, 0)),
                  pl.BlockSpec((tq, dh), lambda b, i: (b * nq + i, 0)),
                  pl.BlockSpec((N_MEM, dh), lambda b, i: (kblk0 + b, 0)),
                  pl.BlockSpec((N_MEM, dh), lambda b, i: (vblk0 + b, vcol)),
                  pl.BlockSpec(wo.shape, lambda b, i: (0, 0))],
        out_specs=pl.BlockSpec((tq, D_MODEL), lambda b, i: (b * nq + i, 0)),
        out_shape=jax.ShapeDtypeStruct((n, D_MODEL), F32),
        compiler_params=_params("arbitrary", "arbitrary"),
        name="mem_attn",
    )(x, q, mk, mv, wo)


def _top16(s, payload=None):
    r = s.shape[0]
    iota = lax.broadcasted_iota(jnp.int32, s.shape, 0)
    vals, idxs = [], []
    for _ in range(PEER_TOPK):
        m = jnp.max(s, axis=0, keepdims=True)
        idx = jnp.min(jnp.where(s == m, iota, r), axis=0, keepdims=True)
        hit = iota == idx
        vals.append(m)
        if payload is None:
            idxs.append(idx)
        else:
            idxs.append(jnp.max(jnp.where(hit, payload, -1), axis=0, keepdims=True))
        s = jnp.where(hit, -jnp.inf, s)
    return vals, idxs


def _peer_topk_kernel(q_ref, sk1_ref, sk2_ref, eid_ref, gate_ref):
    tm = q_ref.shape[0]
    for h in range(PEER_H):
        qh = q_ref[:, h * LANES:(h + 1) * LANES]
        s1 = _dot_nt(sk1_ref[h], qh)
        s2 = _dot_nt(sk2_ref[h], qh)
        v1, i1 = _top16(s1)
        v2, i2 = _top16(s2)
        v2a = jnp.concatenate(v2, axis=0)
        i2a = jnp.concatenate(i2, axis=0)
        cand = jnp.concatenate([jnp.broadcast_to(v1[a], (PEER_TOPK, tm)) + v2a for a in range(PEER_TOPK)], axis=0)
        cidx = jnp.concatenate([jnp.broadcast_to(i1[a], (PEER_TOPK, tm)) * PEER_N_KEYS + i2a
                                for a in range(PEER_TOPK)], axis=0)
        sc, eid = _top16(cand, cidx)
        sc = jnp.concatenate(sc, axis=0)
        e = jnp.exp(sc - sc[0:1])
        gate_ref[h * PEER_TOPK:(h + 1) * PEER_TOPK, :] = e / jnp.sum(e, axis=0, keepdims=True)
        eid_ref[h * PEER_TOPK:(h + 1) * PEER_TOPK, :] = jnp.concatenate(eid, axis=0)


def peer_topk(q, sk1, sk2, tm=128):
    n = q.shape[0]
    rows = PEER_H * PEER_TOPK
    return pl.pallas_call(
        _peer_topk_kernel,
        grid=(n // tm,),
        in_specs=[pl.BlockSpec((tm, D_MODEL), lambda i: (i, 0)),
                  pl.BlockSpec(sk1.shape, lambda i: (0, 0, 0)),
                  pl.BlockSpec(sk2.shape, lambda i: (0, 0, 0))],
        out_specs=[pl.BlockSpec((rows, tm), lambda i: (0, i)),
                   pl.BlockSpec((rows, tm), lambda i: (0, i))],
        out_shape=[jax.ShapeDtypeStruct((rows, n), jnp.int32),
                   jax.ShapeDtypeStruct((rows, n), F32)],
        compiler_params=_params("arbitrary"),
        name="peer_topk",
    )(q, sk1, sk2)


PEER_ROWS = PEER_H * PEER_TOPK
PEER_TOK_BLOCK = 128
PEER_SLOTS = 8


def _gelu_tanh(x):
    return 0.5 * x * (1.0 + jnp.tanh(math.sqrt(2.0 / math.pi) * (x + 0.044715 * x * x * x)))


def _peer_gather_kernel(eid_ref, gate_ref, h_ref, x_ref, tab_hbm, o_ref, buf, sem, *, layer):
    ntok = x_ref.shape[0]
    lookahead = PEER_SLOTS - 1
    tab_ref = tab_hbm.at[layer]

    def row_copy(src_row, slot, k):
        return pltpu.make_async_copy(tab_ref.at[pl.ds(src_row, 1), :],
                                     buf.at[slot, pl.ds(k, 1), :], sem.at[slot])

    def issue(tok):
        slot = tok % PEER_SLOTS
        for k in range(PEER_ROWS):
            row_copy(eid_ref[tok, k], slot, k).start(priority=k % 2)

    def wait(tok):
        slot = tok % PEER_SLOTS
        pltpu.make_async_copy(tab_ref.at[pl.ds(0, PEER_ROWS), :], buf.at[slot], sem.at[slot]).wait()

    for tok in range(lookahead):
        issue(tok)

    lane = lax.broadcasted_iota(jnp.int32, (PEER_ROWS, ntok), 1)

    def body(tok, carry):
        @pl.when(tok + lookahead < ntok)
        def _():
            issue(tok + lookahead)

        wait(tok)
        slot = tok % PEER_SLOTS
        hrow = h_ref[pl.ds(tok, 1), :]
        act = jnp.sum(buf[slot, :, :D_MODEL] * hrow, axis=1, keepdims=True)
        gcol = jnp.sum(jnp.where(lane == tok, gate_ref[...], 0.0), axis=1, keepdims=True)
        w = gcol * _gelu_tanh(act)
        y = jnp.sum(buf[slot, :, D_MODEL:] * w, axis=0, keepdims=True)
        o_ref[pl.ds(tok, 1), :] = x_ref[pl.ds(tok, 1), :] + y
        return carry

    lax.fori_loop(0, ntok, body, 0)


def peer_gather(eid, gate_t, hn, x, table, layer):
    n = x.shape[0]
    tb = PEER_TOK_BLOCK
    return pl.pallas_call(
        functools.partial(_peer_gather_kernel, layer=layer),
        grid=(n // tb,),
        in_specs=[pl.BlockSpec((tb, PEER_ROWS), lambda i: (i, 0), memory_space=pltpu.SMEM),
                  pl.BlockSpec((PEER_ROWS, tb), lambda i: (0, i)),
                  pl.BlockSpec((tb, D_MODEL), lambda i: (i, 0)),
                  pl.BlockSpec((tb, D_MODEL), lambda i: (i, 0)),
                  pl.BlockSpec(memory_space=pl.ANY)],
        out_specs=pl.BlockSpec((tb, D_MODEL), lambda i: (i, 0)),
        out_shape=jax.ShapeDtypeStruct((n, D_MODEL), F32),
        scratch_shapes=[pltpu.VMEM((PEER_SLOTS, PEER_ROWS, 2 * D_MODEL), F32),
                        pltpu.SemaphoreType.DMA((PEER_SLOTS,))],
        compiler_params=_params("arbitrary"),
        name="peer_gather",
    )(eid, gate_t, hn, x, table)


def _rope_tables(pos):
    half = MLA_ROPE // 2
    inv = jnp.exp(-math.log(ROPE_THETA) * jnp.arange(half, dtype=F32) / half)
    ang = pos.astype(F32)[:, None] * inv[None, :]
    cos, sin = jnp.cos(ang), jnp.sin(ang)
    zeros = jnp.zeros_like(cos)
    cos_t = jnp.concatenate([cos, cos, zeros, zeros], axis=1)
    sin_t = jnp.concatenate([-sin, sin, zeros, zeros], axis=1)
    return cos_t, sin_t


def _mix_and_ffn(x, layer, wts, seq_fn):
    proj = norm_matmul(x, wts["ln_mix_g"][layer], wts["w_in"][layer])
    og, olat, extras = seq_fn(proj)
    x = outproj(x, og, olat, wts["w_uv"][layer], wts["w_out"][layer])
    return x, proj, extras


def _peer(x, layer, wts):
    q, hn = norm_matmul(x, wts["ln_peer_g"][layer], wts["peer_wq"][layer], emit_normed=True)
    eid_t, gate_t = peer_topk(q, wts["sk1"][layer], wts["sk2"][layer])
    return peer_gather(eid_t.T, gate_t, hn, x, wts["peer_tab"], layer)


def kernel(x_prompt, x_sample, mem_prompt, cache_mla_ckv, cache_mla_krope, cache_mem_k, cache_mem_v, state_gdn, state_gdn_conv, page_table, ln_mix_g, w_in, gdn_conv_w, gdn_a_log, gdn_dt_bias, gdn_norm_g, mla_q_norm_g, mla_w_uq, mla_kv_norm_g, mla_w_uk, mla_w_uv, w_out, ln_mem_g, mem_norm_g, mem_wq, mem_wk, mem_wv, mem_wo, ln_peer_g, peer_wq, peer_subkeys, peer_u, peer_v, final_norm_g):
    depth = w_in.shape[0]
    b_p, s_len, _ = x_prompt.shape
    b_s, t_new, _ = x_sample.shape
    past_len = page_table.shape[1] * PAGE_SIZE
    n_p, n_s = b_p * s_len, b_s * t_new

    c_qkv, c_z = GDN_QKV, GDN_QKV + 512
    c_a, c_b, c_cq = c_z, c_z + GDN_H, c_z + 2 * GDN_H
    c_ckv = c_cq + MLA_Q_LORA
    c_kr = c_ckv + MLA_KV_LORA
    w_in_r = jnp.concatenate(
        [w_in[:, :, c_cq:c_ckv], w_in[:, :, c_ckv:c_kr], w_in[:, :, c_kr:c_kr + MLA_ROPE],
         w_in[:, :, c_a:c_a + GDN_H], w_in[:, :, c_b:c_b + GDN_H],
         jnp.zeros((depth, D_MODEL, LANES - MLA_ROPE - 2 * GDN_H), F32),
         w_in[:, :, :c_qkv], w_in[:, :, c_qkv:c_z]], axis=2).astype(BF16)
    assert w_in_r.shape[2] == D_PROJ
    wuq_nope = mla_w_uq[:, :, :, :MLA_NOPE].reshape(depth, MLA_Q_LORA, MLA_H * MLA_NOPE)
    wuq_rope = jnp.pad(mla_w_uq[:, :, :, MLA_NOPE:], ((0, 0), (0, 0), (0, 0), (0, LANES - MLA_ROPE)))
    wuq = jnp.concatenate([wuq_nope, wuq_rope.reshape(depth, MLA_Q_LORA, MLA_H * LANES)], axis=2).astype(BF16)
    wuk = jnp.transpose(mla_w_uk, (0, 2, 3, 1)).astype(BF16)
    wuv = jnp.transpose(mla_w_uv, (0, 2, 1, 3)).astype(BF16)
    zk = jnp.zeros((depth, PEER_H, PEER_N_KEYS, LANES // 2), F32)
    wts = {
        "ln_mix_g": ln_mix_g, "w_in": w_in_r, "w_uv": wuv, "w_out": w_out.astype(BF16),
        "ln_peer_g": ln_peer_g, "peer_wq": peer_wq.reshape(depth, D_MODEL, PEER_H * LANES).astype(BF16),
        "sk1": jnp.concatenate([peer_subkeys[:, :, 0], zk], axis=-1).astype(BF16),
        "sk2": jnp.concatenate([zk, peer_subkeys[:, :, 1]], axis=-1).astype(BF16),
        "peer_tab": jnp.concatenate([peer_u, peer_v], axis=-1),
    }
    mem_wkv = jnp.concatenate([mem_wk, mem_wv], axis=2).astype(BF16)
    mem_wq_b = mem_wq.astype(BF16)
    mem_wo_b = mem_wo.astype(BF16)
    dh = MEM_H * MEM_DH
    cache_k2 = cache_mem_k.reshape(depth * b_s * N_MEM, dh)
    cache_v2 = cache_mem_v.reshape(depth * b_s * N_MEM, dh)

    cos_p, sin_p = _rope_tables(jnp.arange(s_len))
    cos_p, sin_p = jnp.tile(cos_p, (b_p, 1)), jnp.tile(sin_p, (b_p, 1))
    cos_s, sin_s = _rope_tables(past_len + jnp.arange(t_new))
    cos_s, sin_s = jnp.tile(cos_s, (b_s, 1)), jnp.tile(sin_s, (b_s, 1))

    xp = x_prompt.reshape(n_p, D_MODEL)
    xs = x_sample.reshape(n_s, D_MODEL)
    mem2 = mem_prompt.reshape(b_p * N_MEM, D_MODEL)
    zero_cbuf = jnp.zeros((b_p, SUBLANES, GDN_QKV), F32)
    zero_state = jnp.zeros((b_p, GDN_H, GDN_DK, LANES), F32)
    tpad = SUBLANES

    outs = {k: [] for k in ("p_ckv", "p_kr", "p_gdn", "p_conv", "p_mk", "p_mv", "s_ckv", "s_kr", "s_gdn", "s_conv")}
    for l in range(depth):
        gdn_w = (gdn_conv_w[l], gdn_a_log[l], gdn_dt_bias[l], gdn_norm_g[l])
        mla_w = (mla_q_norm_g[l], mla_kv_norm_g[l], wuq[l], wuk[l])

        def prompt_seq(proj):
            og, s_new = gdn(proj, zero_cbuf, zero_state, *gdn_w, batch=b_p, rb=CHUNK, valid=CHUNK)
            qcat, kvcat, ckv, kr = mla_prep(proj, cos_p, sin_p, *mla_w)
            olat = mla_flash(qcat, kvcat, batch=b_p, seq=s_len)
            return og, olat, (s_new, ckv, kr)

        xp, proj, (sp, ckvp, krp) = _mix_and_ffn(xp, l, wts, prompt_seq)
        outs["p_gdn"].append(sp)
        outs["p_ckv"].append(ckvp.reshape(b_p, s_len, MLA_KV_LORA))
        outs["p_kr"].append(krp.reshape(b_p, s_len, MLA_ROPE))
        qkv_pre = proj[:, COL_QKV * LANES:COL_QKV * LANES + GDN_QKV].reshape(b_p, s_len, GDN_QKV)
        outs["p_conv"].append(qkv_pre[:, s_len - (GDN_CONV - 1):])
        mkv = norm_matmul(mem2, mem_norm_g[l], mem_wkv[l])
        outs["p_mk"].append(mkv[:, :dh].reshape(b_p, N_MEM, MEM_H, MEM_DH))
        outs["p_mv"].append(mkv[:, dh:].reshape(b_p, N_MEM, MEM_H, MEM_DH))
        qm = norm_matmul(xp, ln_mem_g[l], mem_wq_b[l])
        xp = mem_attn(xp, qm, mkv, mkv, mem_wo_b[l], batch=b_p, tq=256, kblk0=0, vblk0=0, vcol=1)
        xp = _peer(xp, l, wts)

        def sample_seq(proj):
            proj_pad = jnp.pad(proj.reshape(b_s, t_new, D_PROJ), ((0, 0), (0, tpad - t_new), (0, 0)))
            cbuf = jnp.pad(state_gdn_conv[l], ((0, 0), (SUBLANES - (GDN_CONV - 1), 0), (0, 0)))
            og, s_new = gdn(proj_pad.reshape(b_s * tpad, D_PROJ), cbuf, state_gdn[l], *gdn_w,
                            batch=b_s, rb=tpad, valid=t_new)
            og = og.reshape(b_s, tpad, -1)[:, :t_new].reshape(n_s, -1)
            qcat, kvcat, ckv, kr = mla_prep(proj, cos_s, sin_s, *mla_w)
            olat = mla_paged(qcat, kvcat, cache_mla_ckv, cache_mla_krope, page_table, layer=l, t_new=t_new)
            return og, olat, (s_new, ckv, kr)

        xs, proj_s, (ss, ckvs, krs) = _mix_and_ffn(xs, l, wts, sample_seq)
        outs["s_gdn"].append(ss)
        outs["s_ckv"].append(ckvs.reshape(b_s, t_new, MLA_KV_LORA))
        outs["s_kr"].append(krs.reshape(b_s, t_new, MLA_ROPE))
        qkv_pre = proj_s[:, COL_QKV * LANES:COL_QKV * LANES + GDN_QKV].reshape(b_s, t_new, GDN_QKV)
        outs["s_conv"].append(qkv_pre[:, t_new - (GDN_CONV - 1):])
        qm = norm_matmul(xs, ln_mem_g[l], mem_wq_b[l])
        xs_pad = jnp.pad(xs.reshape(b_s, t_new, D_MODEL), ((0, 0), (0, tpad - t_new), (0, 0)))
        qm_pad = jnp.pad(qm.reshape(b_s, t_new, dh), ((0, 0), (0, tpad - t_new), (0, 0)))
        xs_pad = mem_attn(xs_pad.reshape(b_s * tpad, D_MODEL), qm_pad.reshape(b_s * tpad, dh), cache_k2, cache_v2,
                          mem_wo_b[l], batch=b_s, tq=tpad, kblk0=l * b_s, vblk0=l * b_s, vcol=0)
        xs = xs_pad.reshape(b_s, tpad, D_MODEL)[:, :t_new].reshape(n_s, D_MODEL)
        xs = _peer(xs, l, wts)

    y_prompt = final_norm(xp, final_norm_g).reshape(b_p, s_len, D_MODEL)
    y_sample = final_norm(xs, final_norm_g).reshape(b_s, t_new, D_MODEL)
    st = lambda k: jnp.stack(outs[k])
    return (y_prompt, y_sample, st("p_ckv"), st("p_kr"), st("p_gdn"), st("p_conv"), st("p_mk"), st("p_mv"),
            st("s_ckv"), st("s_kr"), st("s_gdn"), st("s_conv"))
```

```python
import functools
import math

import jax
import jax.numpy as jnp
from jax import lax
from jax.experimental import pallas as pl
from jax.experimental.pallas import tpu as pltpu

F32 = jnp.float32
BF16 = jnp.bfloat16

D_MODEL = 1024
GDN_H = 4
GDN_DK = 128
GDN_CONV = 4
GDN_QKV = 1536
MLA_H = 4
MLA_Q_LORA = 384
MLA_KV_LORA = 256
MLA_NOPE = 128
MLA_ROPE = 64
ROPE_THETA = 10000.0
PAGE_SIZE = 128
N_MEM = 256
MEM_H = 4
MEM_DH = 128
PEER_H = 8
PEER_N_KEYS = 128
PEER_TOPK = 16
EPS = 1e-6

LANES = 128
SUBLANES = 8
CHUNK = 128
MLA_CAT = MLA_KV_LORA + LANES
GDN_W = GDN_H * GDN_DK
OFF_CKV = GDN_QKV + GDN_W
OFF_CQ = OFF_CKV + MLA_KV_LORA
OFF_G5 = OFF_CQ + MLA_Q_LORA
D_PROJ = OFF_G5 + LANES
VMEM_LIMIT = 56 * 1024 * 1024

NT_DIMS = (((1,), (1,)), ((), ()))


def _dot(a, b):
    return jnp.dot(a.astype(BF16), b.astype(BF16), preferred_element_type=F32)


def _dot_nt(a, b):
    return lax.dot_general(a.astype(BF16), b.astype(BF16), NT_DIMS, preferred_element_type=F32)


def _sigmoid(x):
    return 1.0 / (1.0 + jnp.exp(-x))


def _rms(x, g):
    return x * lax.rsqrt(jnp.mean(x * x, axis=-1, keepdims=True) + EPS) * g


def _params(*sem):
    return pltpu.CompilerParams(dimension_semantics=sem, vmem_limit_bytes=VMEM_LIMIT)


def _norm_matmul_kernel(x_ref, g_ref, w_ref, o_ref, *h_ref):
    y = _rms(x_ref[...], g_ref[...])
    if h_ref:
        h_ref[0][...] = y
    o_ref[...] = _dot(y, w_ref[...])


def norm_matmul(x, g, w, *, emit_normed=False, tm=256):
    n, d = x.shape
    dout = w.shape[1]
    out_shape = [jax.ShapeDtypeStruct((n, dout), F32)]
    out_specs = [pl.BlockSpec((tm, dout), lambda i: (i, 0))]
    if emit_normed:
        out_shape.append(jax.ShapeDtypeStruct((n, d), F32))
        out_specs.append(pl.BlockSpec((tm, d), lambda i: (i, 0)))
    res = pl.pallas_call(
        _norm_matmul_kernel,
        grid=(n // tm,),
        in_specs=[pl.BlockSpec((tm, d), lambda i: (i, 0)),
                  pl.BlockSpec((1, d), lambda i: (0, 0)),
                  pl.BlockSpec((d, dout), lambda i: (0, 0))],
        out_specs=out_specs,
        out_shape=out_shape,
        compiler_params=_params("arbitrary"),
        name="norm_matmul",
    )(x, g.reshape(1, d), w)
    return res if emit_normed else res[0]


def _final_norm_kernel(x_ref, g_ref, o_ref):
    o_ref[...] = _rms(x_ref[...], g_ref[...])


def final_norm(x, g, tm=256):
    n, d = x.shape
    return pl.pallas_call(
        _final_norm_kernel,
        grid=(n // tm,),
        in_specs=[pl.BlockSpec((tm, d), lambda i: (i, 0)), pl.BlockSpec((1, d), lambda i: (0, 0))],
        out_specs=pl.BlockSpec((tm, d), lambda i: (i, 0)),
        out_shape=jax.ShapeDtypeStruct((n, d), F32),
        compiler_params=_params("arbitrary"),
        name="final_norm",
    )(x, g.reshape(1, d))


def _gdn_kernel(alog_ref, dtb_ref, q_ref, k_ref, v_ref, z_ref, g5_ref, cq_ref, ck_ref, cv_ref,
                wq_ref, wk_ref, wv_ref, ng_ref, s0_ref, o_ref, sout_ref,
                s_scr, carq, cark, carv, *, rb, valid, nt):
    t = pl.program_id(1)

    @pl.when(t == 0)
    def _():
        carq[...] = cq_ref[0]
        cark[...] = ck_ref[0]
        carv[...] = cv_ref[0]
        s_scr[...] = s0_ref[0]

    def conv(x_ref, car, w_ref):
        x = x_ref[...]
        cat = jnp.concatenate([car[...], x], axis=0)
        w = w_ref[...]
        y = cat[SUBLANES:SUBLANES + rb] * w[3:4]
        for i in range(1, GDN_CONV):
            y = y + cat[SUBLANES - i:SUBLANES - i + rb] * w[3 - i:4 - i]
        car[...] = x[rb - SUBLANES:rb]
        return y * _sigmoid(y)

    def l2n(x):
        return x * lax.rsqrt(jnp.sum(x * x, axis=-1, keepdims=True) + EPS)

    q_all = conv(q_ref, carq, wq_ref)
    k_all = conv(k_ref, cark, wk_ref)
    v_all = conv(v_ref, carv, wv_ref)
    g5 = g5_ref[...]
    lane = lax.broadcasted_iota(jnp.int32, g5.shape, 1)
    zz = z_ref[...]
    zgate = zz * _sigmoid(zz)

    row = lax.broadcasted_iota(jnp.int32, (CHUNK, CHUNK), 0)
    col = lax.broadcasted_iota(jnp.int32, (CHUNK, CHUNK), 1)
    causal = row >= col
    strict = row > col
    eye = (row == col).astype(F32)
    tri = causal.astype(BF16)
    n_doublings = max(int(math.ceil(math.log2(valid))) - 1, 0)
    heads = range(GDN_H)

    def split2(a):
        hi = a.astype(BF16)
        return hi, (a - hi.astype(F32)).astype(BF16)

    def dot_split(a, b):
        (ah, al), (bh, bl) = a, b
        d = functools.partial(jnp.dot, preferred_element_type=F32)
        return d(ah, bh) + (d(ah, bl) + d(al, bh))

    def prep(h):
        hs = slice(h * GDN_DK, (h + 1) * GDN_DK)
        q = l2n(q_all[:, hs]) * (GDN_DK ** -0.5)
        k = l2n(k_all[:, hs])
        v = v_all[:, hs]
        ga = jnp.sum(jnp.where(lane == MLA_ROPE + h, g5, 0.0), axis=1, keepdims=True)
        gb = jnp.sum(jnp.where(lane == MLA_ROPE + GDN_H + h, g5, 0.0), axis=1, keepdims=True)
        beta = _sigmoid(gb)
        sp = ga + dtb_ref[h]
        softplus = jnp.maximum(sp, 0.0) + jnp.log(1.0 + jnp.exp(-jnp.abs(sp)))
        g = -jnp.exp(jnp.full((rb, 1), alog_ref[h], F32)) * softplus
        if valid < rb:
            rvalid = lax.broadcasted_iota(jnp.int32, (rb, 1), 0) < valid
            beta = jnp.where(rvalid, beta, 0.0)
            g = jnp.where(rvalid, g, 0.0)
        if rb < CHUNK:
            def padr(a):
                return jnp.concatenate([a, jnp.zeros((CHUNK - rb, a.shape[1]), F32)], axis=0)
            q, k, v, beta, g = padr(q), padr(k), padr(v), padr(beta), padr(g)
        return q, k, v, beta, g

    q, k, v, beta, g = zip(*[prep(h) for h in heads])

    def cumsum_rows(gh):
        g_b = jnp.broadcast_to(gh, (CHUNK, CHUNK))
        p0 = g_b.astype(BF16)
        r1 = g_b - p0.astype(F32)
        p1 = r1.astype(BF16)
        p2 = (r1 - p1.astype(F32)).astype(BF16)
        d = functools.partial(jnp.dot, preferred_element_type=F32)
        return d(tri, p0) + (d(tri, p1) + d(tri, p2))

    gc = [cumsum_rows(g[h]) for h in heads]
    gr = [gc[h].T for h in heads]
    decay = [jnp.where(causal, jnp.exp(jnp.where(causal, gc[h] - gr[h], 0.0)), 0.0) for h in heads]
    kb = [k[h] * beta[h] for h in heads]
    low = [jnp.where(strict, _dot_nt(kb[h], k[h]) * decay[h], 0.0) for h in heads]
    m = [-low[h] for h in heads]
    tinv = [eye + m[h] for h in heads]
    for _ in range(n_doublings):
        ms = [split2(m[h]) for h in heads]
        m = [dot_split(ms[h], ms[h]) for h in heads]
        ms = [split2(m[h]) for h in heads]
        tinv = [tinv[h] + dot_split(split2(tinv[h]), ms[h]) for h in heads]
    eg = [jnp.exp(gc[h]) for h in heads]
    u = [_dot(tinv[h], v[h] * beta[h]) for h in heads]
    w = [_dot(tinv[h], kb[h] * eg[h]) for h in heads]
    qk = [_dot_nt(q[h], k[h]) * decay[h] for h in heads]
    g_last = [gc[h][CHUNK - 1:CHUNK, :] for h in heads]
    s = [s_scr[h] for h in heads]
    v_new = [u[h] - _dot(w[h], s[h]) for h in heads]
    o = [_dot(q[h] * eg[h], s[h]) + _dot(qk[h], v_new[h]) for h in heads]
    k_dec = [k[h] * jnp.exp(g_last[h] - gc[h]) for h in heads]
    for h in heads:
        s_scr[h] = s[h] * jnp.exp(g_last[h]) + _dot(k_dec[h].T, v_new[h])
        hs = slice(h * GDN_DK, (h + 1) * GDN_DK)
        o_ref[:, hs] = _rms(o[h][:rb], ng_ref[...]) * zgate[:, hs]

    @pl.when(t == nt - 1)
    def _():
        sout_ref[0] = s_scr[...]


def gdn(proj, cbuf, s0, conv_w, a_log, dt_bias, norm_g, *, batch, rb, valid):
    rows = proj.shape[0]
    nt = rows // (batch * rb)
    kern = functools.partial(_gdn_kernel, rb=rb, valid=valid, nt=nt)

    def colblk(c):
        return pl.BlockSpec((rb, GDN_W), lambda b, t, *_: (b * nt + t, c))

    def cblk(c):
        return pl.BlockSpec((1, SUBLANES, GDN_W), lambda b, t, *_: (b, 0, c))

    def wblk(c):
        return pl.BlockSpec((GDN_CONV, GDN_W), lambda b, t, *_: (0, c))

    state_spec = pl.BlockSpec((1, GDN_H, GDN_DK, LANES), lambda b, t, *_: (b, 0, 0, 0))
    grid_spec = pltpu.PrefetchScalarGridSpec(
        num_scalar_prefetch=2,
        grid=(batch, nt),
        in_specs=[colblk(0), colblk(1), colblk(2), colblk(3),
                  pl.BlockSpec((rb, LANES), lambda b, t, *_: (b * nt + t, OFF_G5 // LANES)),
                  cblk(0), cblk(1), cblk(2), wblk(0), wblk(1), wblk(2),
                  pl.BlockSpec((1, LANES), lambda b, t, *_: (0, 0)),
                  state_spec],
        out_specs=[pl.BlockSpec((rb, GDN_W), lambda b, t, *_: (b * nt + t, 0)), state_spec],
        scratch_shapes=[pltpu.VMEM((GDN_H, GDN_DK, LANES), F32)] + [pltpu.VMEM((SUBLANES, GDN_W), F32)] * 3,
    )
    return pl.pallas_call(
        kern, grid_spec=grid_spec,
        out_shape=[jax.ShapeDtypeStruct((rows, GDN_W), F32),
                   jax.ShapeDtypeStruct((batch, GDN_H, GDN_DK, LANES), F32)],
        compiler_params=_params("arbitrary", "arbitrary"),
        name="gdn",
    )(a_log, dt_bias, proj, proj, proj, proj, proj, cbuf, cbuf, cbuf, conv_w, conv_w, conv_w,
      norm_g.reshape(1, LANES), s0)


def _rope(x, cos, sin_signed):
    lane = lax.broadcasted_iota(jnp.int32, x.shape, 1)
    first_half = (lane % MLA_ROPE) < (MLA_ROPE // 2)
    rot = jnp.where(first_half, pltpu.roll(x, LANES - MLA_ROPE // 2, 1), pltpu.roll(x, MLA_ROPE // 2, 1))
    return x * cos + rot * sin_signed


def _mla_prep_kernel(ckv_in_ref, cq_in_ref, g5_ref, cos_ref, sin_ref, qg_ref, kvg_ref, wuq_ref, wuk_ref,
                     qcat_ref, kvcat_ref, ckv_ref, kr_ref):
    cos = cos_ref[...]
    sin = sin_ref[...]
    cq = _rms(cq_in_ref[...], qg_ref[...])
    ckv = _rms(ckv_in_ref[...], kvg_ref[...])
    kr = _rope(g5_ref[...], cos, sin)
    ckv_ref[...] = ckv
    kr_ref[...] = kr[:, :MLA_ROPE]
    kvcat_ref[:, :MLA_KV_LORA] = ckv.astype(BF16)
    kvcat_ref[:, MLA_KV_LORA:] = kr.astype(BF16)
    qh = _dot(cq, wuq_ref[...])
    scale = (MLA_NOPE + MLA_ROPE) ** -0.5
    for h in range(MLA_H):
        q_lat = _dot(qh[:, h * LANES:(h + 1) * LANES], wuk_ref[h])
        q_rope = _rope(qh[:, (MLA_H + h) * LANES:(MLA_H + h + 1) * LANES], cos, sin)
        qcat_ref[:, h * MLA_CAT:h * MLA_CAT + MLA_KV_LORA] = (q_lat * scale).astype(BF16)
        qcat_ref[:, h * MLA_CAT + MLA_KV_LORA:(h + 1) * MLA_CAT] = (q_rope * scale).astype(BF16)


def mla_prep(proj, cos, sin, qg, kvg, wuq, wuk, tm=256):
    n = proj.shape[0]
    return pl.pallas_call(
        _mla_prep_kernel,
        grid=(n // tm,),
        in_specs=[pl.BlockSpec((tm, MLA_KV_LORA), lambda i: (i, OFF_CKV // MLA_KV_LORA)),
                  pl.BlockSpec((tm, MLA_Q_LORA), lambda i: (i, OFF_CQ // MLA_Q_LORA)),
                  pl.BlockSpec((tm, LANES), lambda i: (i, OFF_G5 // LANES)),
                  pl.BlockSpec((tm, LANES), lambda i: (i, 0)),
                  pl.BlockSpec((tm, LANES), lambda i: (i, 0)),
                  pl.BlockSpec((1, MLA_Q_LORA), lambda i: (0, 0)),
                  pl.BlockSpec((1, MLA_KV_LORA), lambda i: (0, 0)),
                  pl.BlockSpec(wuq.shape, lambda i: (0, 0)),
                  pl.BlockSpec(wuk.shape, lambda i: (0, 0, 0))],
        out_specs=[pl.BlockSpec((tm, MLA_H * MLA_CAT), lambda i: (i, 0)),
                   pl.BlockSpec((tm, MLA_CAT), lambda i: (i, 0)),
                   pl.BlockSpec((tm, MLA_KV_LORA), lambda i: (i, 0)),
                   pl.BlockSpec((tm, MLA_ROPE), lambda i: (i, 0))],
        out_shape=[jax.ShapeDtypeStruct((n, MLA_H * MLA_CAT), BF16),
                   jax.ShapeDtypeStruct((n, MLA_CAT), BF16),
                   jax.ShapeDtypeStruct((n, MLA_KV_LORA), F32),
                   jax.ShapeDtypeStruct((n, MLA_ROPE), F32)],
        compiler_params=_params("arbitrary"),
        name="mla_prep",
    )(proj, proj, proj, cos, sin, qg.reshape(1, -1), kvg.reshape(1, -1), wuq, wuk)


def _flash_kernel(q_ref, kv_ref, o_ref, m_scr, l_scr, acc_scr, *, tq, tk, nk):
    qi = pl.program_id(1)
    kj = pl.program_id(2)

    @pl.when(kj == 0)
    def _():
        m_scr[...] = jnp.full(m_scr.shape, -jnp.inf, F32)
        l_scr[...] = jnp.zeros(l_scr.shape, F32)
        acc_scr[...] = jnp.zeros(acc_scr.shape, F32)

    @pl.when(kj * tk <= qi * tq + tq - 1)
    def _():
        kv = kv_ref[...]
        qpos = qi * tq + lax.broadcasted_iota(jnp.int32, (tq, tk), 0)
        kpos = kj * tk + lax.broadcasted_iota(jnp.int32, (tq, tk), 1)
        mask = kpos <= qpos
        heads = range(MLA_H)
        s = [jnp.where(mask, lax.dot_general(q_ref[:, h * MLA_CAT:(h + 1) * MLA_CAT], kv, NT_DIMS,
                                             preferred_element_type=F32), -jnp.inf) for h in heads]
        m_prev = [m_scr[h] for h in heads]
        m_new = [jnp.maximum(m_prev[h], jnp.max(s[h], axis=1, keepdims=True)) for h in heads]
        p = [jnp.exp(s[h] - m_new[h]) for h in heads]
        alpha = [jnp.exp(m_prev[h] - m_new[h]) for h in heads]
        pv = [jnp.dot(p[h].astype(BF16), kv[:, :MLA_KV_LORA], preferred_element_type=F32) for h in heads]
        for h in heads:
            l_scr[h] = alpha[h] * l_scr[h] + jnp.sum(p[h], axis=1, keepdims=True)
            acc_scr[h] = alpha[h] * acc_scr[h] + pv[h]
            m_scr[h] = m_new[h]

    @pl.when(kj == nk - 1)
    def _():
        for h in range(MLA_H):
            o_ref[:, h * MLA_KV_LORA:(h + 1) * MLA_KV_LORA] = acc_scr[h] / l_scr[h]


def mla_flash(qcat, kvcat, *, batch, seq, tq=256, tk=256):
    nq, nk = seq // tq, seq // tk
    kern = functools.partial(_flash_kernel, tq=tq, tk=tk, nk=nk)

    def kv_map(b, i, j):
        last = (i * tq + tq - 1) // tk
        return (b * nk + jnp.minimum(j, last), 0)

    return pl.pallas_call(
        kern,
        grid=(batch, nq, nk),
        in_specs=[pl.BlockSpec((tq, MLA_H * MLA_CAT), lambda b, i, j: (b * nq + i, 0)),
                  pl.BlockSpec((tk, MLA_CAT), kv_map)],
        out_specs=pl.BlockSpec((tq, MLA_H * MLA_KV_LORA), lambda b, i, j: (b * nq + i, 0)),
        out_shape=jax.ShapeDtypeStruct((batch * seq, MLA_H * MLA_KV_LORA), F32),
        scratch_shapes=[pltpu.VMEM((MLA_H, tq, 1), F32), pltpu.VMEM((MLA_H, tq, 1), F32),
                        pltpu.VMEM((MLA_H, tq, MLA_KV_LORA), F32)],
        compiler_params=_params("arbitrary", "arbitrary", "arbitrary"),
        name="mla_flash",
    )(qcat, kvcat)


PAGES_PER_STEP = 8


def _mla_paged_kernel(pt_ref, q_ref, kvn_ref, *refs, t_new, n_steps):
    ckv_refs = refs[:PAGES_PER_STEP]
    kr_refs = refs[PAGES_PER_STEP:2 * PAGES_PER_STEP]
    o_ref, m_scr, l_scr, acc_scr = refs[2 * PAGES_PER_STEP:]
    j = pl.program_id(1)

    @pl.when(j == 0)
    def _():
        m_scr[...] = jnp.full(m_scr.shape, -jnp.inf, F32)
        l_scr[...] = jnp.zeros(l_scr.shape, F32)
        acc_scr[...] = jnp.zeros(acc_scr.shape, F32)

    q = q_ref[0]
    q_lat = q[:, :MLA_KV_LORA]
    q_rope = q[:, MLA_KV_LORA:MLA_KV_LORA + MLA_ROPE]

    def update(s_list, v_list):
        m_prev = m_scr[...]
        m_new = m_prev
        for s in s_list:
            m_new = jnp.maximum(m_new, jnp.max(s, axis=1, keepdims=True))
        alpha = jnp.exp(m_prev - m_new)
        l_new = alpha * l_scr[...]
        acc = alpha * acc_scr[...]
        for s, vv in zip(s_list, v_list):
            p = jnp.exp(s - m_new)
            l_new = l_new + jnp.sum(p, axis=1, keepdims=True)
            acc = acc + jnp.dot(p.astype(BF16), vv, preferred_element_type=F32)
        m_scr[...] = m_new
        l_scr[...] = l_new
        acc_scr[...] = acc

    s_list, v_list = [], []
    for i in range(PAGES_PER_STEP):
        ck = ckv_refs[i][0, 0].astype(BF16)
        kr = kr_refs[i][0, 0].astype(BF16)
        s = (lax.dot_general(q_lat, ck, NT_DIMS, preferred_element_type=F32)
             + lax.dot_general(q_rope, kr, NT_DIMS, preferred_element_type=F32))
        s_list.append(s)
        v_list.append(ck)
    update(s_list, v_list)

    @pl.when(j == n_steps - 1)
    def _():
        kvn = kvn_ref[0]
        kvn = jnp.concatenate([kvn, jnp.zeros((LANES - kvn.shape[0], MLA_CAT), BF16)], axis=0)
        s = lax.dot_general(q, kvn, NT_DIMS, preferred_element_type=F32)
        nrow = q.shape[0]
        tok = lax.broadcasted_iota(jnp.int32, (nrow, LANES), 0) // MLA_H
        cpos = lax.broadcasted_iota(jnp.int32, (nrow, LANES), 1)
        s = jnp.where((cpos <= tok) & (cpos < t_new), s, -jnp.inf)
        update([s], [kvn[:, :MLA_KV_LORA]])
        o_ref[0] = acc_scr[...] / l_scr[...]


def mla_paged(qcat, kvcat_new, cache_ckv, cache_kr, page_table, *, layer, t_new):
    nb, n_pages = page_table.shape
    n_steps = n_pages // PAGES_PER_STEP
    nrow = t_new * MLA_H
    q3 = qcat.reshape(nb, nrow, MLA_CAT)
    kvn = jnp.pad(kvcat_new.reshape(nb, t_new, MLA_CAT), ((0, 0), (0, 2 * SUBLANES - t_new), (0, 0)))
    kern = functools.partial(_mla_paged_kernel, t_new=t_new, n_steps=n_steps)

    def page_spec(width, i):
        return pl.BlockSpec((1, 1, PAGE_SIZE, width),
                            lambda b, j, pt: (layer, pt[b, j * PAGES_PER_STEP + i], 0, 0))

    grid_spec = pltpu.PrefetchScalarGridSpec(
        num_scalar_prefetch=1,
        grid=(nb, n_steps),
        in_specs=[pl.BlockSpec((1, nrow, MLA_CAT), lambda b, j, pt: (b, 0, 0)),
                  pl.BlockSpec((1, 2 * SUBLANES, MLA_CAT), lambda b, j, pt: (b, 0, 0))]
                 + [page_spec(MLA_KV_LORA, i) for i in range(PAGES_PER_STEP)]
                 + [page_spec(MLA_ROPE, i) for i in range(PAGES_PER_STEP)],
        out_specs=pl.BlockSpec((1, nrow, MLA_KV_LORA), lambda b, j, pt: (b, 0, 0)),
        scratch_shapes=[pltpu.VMEM((nrow, 1), F32), pltpu.VMEM((nrow, 1), F32),
                        pltpu.VMEM((nrow, MLA_KV_LORA), F32)],
    )
    out = pl.pallas_call(
        kern, grid_spec=grid_spec,
        out_shape=jax.ShapeDtypeStruct((nb, nrow, MLA_KV_LORA), F32),
        compiler_params=_params("arbitrary", "arbitrary"),
        name="mla_paged",
    )(page_table, q3, kvn, *([cache_ckv] * PAGES_PER_STEP), *([cache_kr] * PAGES_PER_STEP))
    return out.reshape(nb * t_new, MLA_H * MLA_KV_LORA)


def _outproj_kernel(x_ref, og_ref, ol_ref, wuv_ref, wo_ref, o_ref):
    parts = [og_ref[...]]
    for h in range(MLA_H):
        parts.append(_dot(ol_ref[:, h * MLA_KV_LORA:(h + 1) * MLA_KV_LORA], wuv_ref[h]))
    mixed = jnp.concatenate(parts, axis=1)
    o_ref[...] = x_ref[...] + _dot(mixed, wo_ref[...])


def outproj(x, og, olat, wuv, wout, tm=256):
    n = x.shape[0]
    return pl.pallas_call(
        _outproj_kernel,
        grid=(n // tm,),
        in_specs=[pl.BlockSpec((tm, D_MODEL), lambda i: (i, 0)),
                  pl.BlockSpec((tm, og.shape[1]), lambda i: (i, 0)),
                  pl.BlockSpec((tm, olat.shape[1]), lambda i: (i, 0)),
                  pl.BlockSpec(wuv.shape, lambda i: (0, 0, 0)),
                  pl.BlockSpec(wout.shape, lambda i: (0, 0))],
        out_specs=pl.BlockSpec((tm, D_MODEL), lambda i: (i, 0)),
        out_shape=jax.ShapeDtypeStruct((n, D_MODEL), F32),
        compiler_params=_params("arbitrary"),
        name="outproj",
    )(x, og, olat, wuv, wout)


def _mem_attn_kernel(x_ref, q_ref, mk_ref, mv_ref, wo_ref, o_ref):
    parts = []
    scale = MEM_DH ** -0.5
    for h in range(MEM_H):
        sl = slice(h * MEM_DH, (h + 1) * MEM_DH)
        s = _dot_nt(q_ref[:, sl], mk_ref[:, sl]) * scale
        s = s - jnp.max(s, axis=1, keepdims=True)
        p = jnp.exp(s)
        p = p / jnp.sum(p, axis=1, keepdims=True)
        parts.append(_dot(p, mv_ref[:, sl]))
    o_ref[...] = x_ref[...] + _dot(jnp.concatenate(parts, axis=1), wo_ref[...])


def mem_attn(x, q, mk, mv, wo, *, batch, tq, kblk0, vblk0, vcol):
    n = x.shape[0]
    nq = n // (batch * tq)
    dh = MEM_H * MEM_DH
    return pl.pallas_call(
        _mem_attn_kernel,
        grid=(batch, nq),
        in_specs=[pl.BlockSpec((tq, D_MODEL), lambda b, i: (b * nq + i, 0)),
                  pl.BlockSpec((tq, dh), lambda b, i: (b * nq + i, 0)),
                  pl.BlockSpec((N_MEM, dh), lambda b, i: (kblk0 + b, 0)),
                  pl.BlockSpec((N_MEM, dh), lambda b, i: (vblk0 + b, vcol)),
                  pl.BlockSpec(wo.shape, lambda b, i: (0, 0))],
        out_specs=pl.BlockSpec((tq, D_MODEL), lambda b, i: (b * nq + i, 0)),
        out_shape=jax.ShapeDtypeStruct((n, D_MODEL), F32),
        compiler_params=_params("arbitrary", "arbitrary"),
        name="mem_attn",
    )(x, q, mk, mv, wo)


def _top16(s, payload=None):
    r = s.shape[0]
    iota = lax.broadcasted_iota(jnp.int32, s.shape, 0)
    vals, idxs = [], []
    for _ in range(PEER_TOPK):
        m = jnp.max(s, axis=0, keepdims=True)
        idx = jnp.min(jnp.where(s == m, iota, r), axis=0, keepdims=True)
        hit = iota == idx
        vals.append(m)
        if payload is None:
            idxs.append(idx)
        else:
            idxs.append(jnp.max(jnp.where(hit, payload, -1), axis=0, keepdims=True))
        s = jnp.where(hit, -jnp.inf, s)
    return vals, idxs


_PAIR_ROWS = [PEER_TOPK // (a + 1) for a in range(PEER_TOPK)]


def _peer_topk_kernel(q_ref, sk1_ref, sk2_ref, eid_ref, gate_ref):
    tm = q_ref.shape[0]
    row8 = lax.broadcasted_iota(jnp.int32, (SUBLANES, tm), 0)
    n_wide = sum(1 for r in _PAIR_ROWS if r > 1)
    for h in range(PEER_H):
        qh = q_ref[:, h * LANES:(h + 1) * LANES]
        s1 = _dot_nt(sk1_ref[h], qh)
        s2 = _dot_nt(sk2_ref[h], qh)
        v1, i1 = _top16(s1)
        v2, i2 = _top16(s2)
        v2a = jnp.concatenate(v2, axis=0)
        i2a = jnp.concatenate(i2, axis=0)
        cand, cidx = [], []
        for a in range(n_wide):
            nb = _PAIR_ROWS[a]
            rows = -(-nb // SUBLANES) * SUBLANES
            va = jnp.broadcast_to(v1[a], (rows, tm)) + v2a[:rows]
            ia = jnp.broadcast_to(i1[a], (rows, tm)) * PEER_N_KEYS + i2a[:rows]
            if nb < rows:
                va = jnp.where(row8 < nb, va, -jnp.inf)
            cand.append(va)
            cidx.append(ia)
        cand.append(jnp.concatenate(v1[n_wide:], axis=0) + jnp.broadcast_to(v2[0], (PEER_TOPK - n_wide, tm)))
        cidx.append(jnp.concatenate(i1[n_wide:], axis=0) * PEER_N_KEYS
                    + jnp.broadcast_to(i2[0], (PEER_TOPK - n_wide, tm)))
        sc, eid = _top16(jnp.concatenate(cand, axis=0), jnp.concatenate(cidx, axis=0))
        sc = jnp.concatenate(sc, axis=0)
        e = jnp.exp(sc - sc[0:1])
        gate_ref[h * PEER_TOPK:(h + 1) * PEER_TOPK, :] = e / jnp.sum(e, axis=0, keepdims=True)
        eid_ref[h * PEER_TOPK:(h + 1) * PEER_TOPK, :] = jnp.concatenate(eid, axis=0)


def peer_topk(q, sk1, sk2, tm=128):
    n = q.shape[0]
    rows = PEER_H * PEER_TOPK
    return pl.pallas_call(
        _peer_topk_kernel,
        grid=(n // tm,),
        in_specs=[pl.BlockSpec((tm, D_MODEL), lambda i: (i, 0)),
                  pl.BlockSpec(sk1.shape, lambda i: (0, 0, 0)),
                  pl.BlockSpec(sk2.shape, lambda i: (0, 0, 0))],
        out_specs=[pl.BlockSpec((rows, tm), lambda i: (0, i)),
                   pl.BlockSpec((rows, tm), lambda i: (0, i))],
        out_shape=[jax.ShapeDtypeStruct((rows, n), jnp.int32),
                   jax.ShapeDtypeStruct((rows, n), F32)],
        compiler_params=_params("arbitrary"),
        name="peer_topk",
    )(q, sk1, sk2)


PEER_ROWS = PEER_H * PEER_TOPK
PEER_TOK_BLOCK = 128
PEER_SLOTS = 8
PEER_CHUNKS = 2 * D_MODEL // LANES


def _gelu_tanh(x):
    return 0.5 * x * (1.0 + jnp.tanh(math.sqrt(2.0 / math.pi) * (x + 0.044715 * x * x * x)))


def _peer_gather_kernel(eid_ref, gate_ref, h_ref, x_ref, tab_ref, o_ref, *scratch, nsteps):
    bufs, sem = scratch[:PEER_SLOTS], scratch[PEER_SLOTS]
    ntok = x_ref.shape[0]
    lookahead = PEER_SLOTS - 1
    step = pl.program_id(0)
    half = PEER_CHUNKS // 2

    def issue(tok, slot):
        for k in range(PEER_ROWS):
            pltpu.make_async_copy(tab_ref.at[eid_ref[0, tok, k]], bufs[slot].at[:, pl.ds(k, 1), :],
                                  sem.at[slot]).start(priority=k % 2)

    def wait(slot):
        pltpu.make_async_copy(bufs[slot], bufs[slot], sem.at[slot]).wait()

    lane_row = lax.broadcasted_iota(jnp.int32, (1, ntok), 1)

    def compute(tok, slot):
        buf = bufs[slot]
        hrow = h_ref[pl.ds(tok, 1), :]
        acc = buf[0] * hrow[:, 0:LANES]
        for c in range(1, half):
            acc = acc + buf[c] * hrow[:, c * LANES:(c + 1) * LANES]
        act = jnp.sum(acc, axis=1, keepdims=True)
        gcol = jnp.sum(jnp.where(lane_row == tok, gate_ref[...], 0.0), axis=1, keepdims=True)
        w = gcol * _gelu_tanh(act)
        ys = [jnp.sum(buf[half + c] * w, axis=0, keepdims=True) for c in range(half)]
        o_ref[pl.ds(tok, 1), :] = x_ref[pl.ds(tok, 1), :] + jnp.concatenate(ys, axis=1)

    @pl.when(step == 0)
    def _():
        for t in range(lookahead):
            issue(t, t)

    def group(g, carry):
        for j in range(PEER_SLOTS):
            tok = g * PEER_SLOTS + j
            wait(j)
            issue(tok + lookahead, (j + lookahead) % PEER_SLOTS)
            compute(tok, j)
        return carry

    lax.fori_loop(0, ntok // PEER_SLOTS, group, 0)

    @pl.when(step == nsteps - 1)
    def _():
        for t in range(lookahead):
            wait(t)


def peer_gather(eid, gate_t, hn, x, table):
    n = x.shape[0]
    tb = PEER_TOK_BLOCK
    nsteps = n // tb
    eid3 = eid.reshape(nsteps, tb, PEER_ROWS)
    head = jnp.concatenate([eid3[1:, :PEER_SLOTS], eid3[-1:, :PEER_SLOTS]], axis=0)
    eid_ext = jnp.concatenate([eid3, head], axis=1)
    return pl.pallas_call(
        functools.partial(_peer_gather_kernel, nsteps=nsteps),
        grid=(nsteps,),
        in_specs=[pl.BlockSpec((1, tb + PEER_SLOTS, PEER_ROWS), lambda i: (i, 0, 0), memory_space=pltpu.SMEM),
                  pl.BlockSpec((PEER_ROWS, tb), lambda i: (0, i)),
                  pl.BlockSpec((tb, D_MODEL), lambda i: (i, 0)),
                  pl.BlockSpec((tb, D_MODEL), lambda i: (i, 0)),
                  pl.BlockSpec(memory_space=pl.ANY)],
        out_specs=pl.BlockSpec((tb, D_MODEL), lambda i: (i, 0)),
        out_shape=jax.ShapeDtypeStruct((n, D_MODEL), F32),
        scratch_shapes=[pltpu.VMEM((PEER_CHUNKS, PEER_ROWS, LANES), F32)] * PEER_SLOTS
                       + [pltpu.SemaphoreType.DMA((PEER_SLOTS,))],
        compiler_params=_params("arbitrary"),
        name="peer_gather",
    )(eid_ext, gate_t, hn, x, table)


def _rope_tables(pos):
    half = MLA_ROPE // 2
    inv = jnp.exp(-math.log(ROPE_THETA) * jnp.arange(half, dtype=F32) / half)
    ang = pos.astype(F32)[:, None] * inv[None, :]
    cos, sin = jnp.cos(ang), jnp.sin(ang)
    zeros = jnp.zeros_like(cos)
    cos_t = jnp.concatenate([cos, cos, zeros, zeros], axis=1)
    sin_t = jnp.concatenate([-sin, sin, zeros, zeros], axis=1)
    return cos_t, sin_t


def _mix_and_ffn(x, layer, wts, seq_fn):
    proj = norm_matmul(x, wts["ln_mix_g"][layer], wts["w_in"][layer])
    og, olat, extras = seq_fn(proj)
    x = outproj(x, og, olat, wts["w_uv"][layer], wts["w_out"][layer])
    return x, proj, extras


def _peer(x, layer, wts):
    q, hn = norm_matmul(x, wts["ln_peer_g"][layer], wts["peer_wq"][layer], emit_normed=True)
    eid_t, gate_t = peer_topk(q, wts["sk1"][layer], wts["sk2"][layer])
    eid = eid_t.T + layer * wts["n_experts"]
    return peer_gather(eid, gate_t, hn, x, wts["peer_tab"])


def kernel(x_prompt, x_sample, mem_prompt, cache_mla_ckv, cache_mla_krope, cache_mem_k, cache_mem_v, state_gdn, state_gdn_conv, page_table, ln_mix_g, w_in, gdn_conv_w, gdn_a_log, gdn_dt_bias, gdn_norm_g, mla_q_norm_g, mla_w_uq, mla_kv_norm_g, mla_w_uk, mla_w_uv, w_out, ln_mem_g, mem_norm_g, mem_wq, mem_wk, mem_wv, mem_wo, ln_peer_g, peer_wq, peer_subkeys, peer_u, peer_v, final_norm_g):
    depth = w_in.shape[0]
    b_p, s_len, _ = x_prompt.shape
    b_s, t_new, _ = x_sample.shape
    past_len = page_table.shape[1] * PAGE_SIZE
    n_p, n_s = b_p * s_len, b_s * t_new

    c_z = GDN_QKV + GDN_W
    c_a, c_b, c_cq = c_z, c_z + GDN_H, c_z + 2 * GDN_H
    c_ckv = c_cq + MLA_Q_LORA
    c_kr = c_ckv + MLA_KV_LORA
    w_in_r = jnp.concatenate(
        [w_in[:, :, :c_z], w_in[:, :, c_ckv:c_kr], w_in[:, :, c_cq:c_ckv], w_in[:, :, c_kr:c_kr + MLA_ROPE],
         w_in[:, :, c_a:c_a + GDN_H], w_in[:, :, c_b:c_b + GDN_H],
         jnp.zeros((depth, D_MODEL, LANES - MLA_ROPE - 2 * GDN_H), F32)], axis=2).astype(BF16)
    assert w_in_r.shape[2] == D_PROJ
    wuq_nope = mla_w_uq[:, :, :, :MLA_NOPE].reshape(depth, MLA_Q_LORA, MLA_H * MLA_NOPE)
    wuq_rope = jnp.pad(mla_w_uq[:, :, :, MLA_NOPE:], ((0, 0), (0, 0), (0, 0), (0, LANES - MLA_ROPE)))
    wuq = jnp.concatenate([wuq_nope, wuq_rope.reshape(depth, MLA_Q_LORA, MLA_H * LANES)], axis=2).astype(BF16)
    wuk = jnp.transpose(mla_w_uk, (0, 2, 3, 1)).astype(BF16)
    wuv = jnp.transpose(mla_w_uv, (0, 2, 1, 3)).astype(BF16)
    zk = jnp.zeros((depth, PEER_H, PEER_N_KEYS, LANES // 2), F32)
    wts = {
        "ln_mix_g": ln_mix_g, "w_in": w_in_r, "w_uv": wuv, "w_out": w_out.astype(BF16),
        "ln_peer_g": ln_peer_g, "peer_wq": peer_wq.reshape(depth, D_MODEL, PEER_H * LANES).astype(BF16),
        "sk1": jnp.concatenate([peer_subkeys[:, :, 0], zk], axis=-1).astype(BF16),
        "sk2": jnp.concatenate([zk, peer_subkeys[:, :, 1]], axis=-1).astype(BF16),
        "peer_tab": jnp.concatenate([peer_u, peer_v], axis=-1).reshape(-1, PEER_CHUNKS, 1, LANES),
        "n_experts": peer_u.shape[1],
    }
    mem_wkv = jnp.concatenate([mem_wk, mem_wv], axis=2).astype(BF16)
    mem_wq_b = mem_wq.astype(BF16)
    mem_wo_b = mem_wo.astype(BF16)
    dh = MEM_H * MEM_DH
    cache_k2 = cache_mem_k.reshape(depth * b_s * N_MEM, dh)
    cache_v2 = cache_mem_v.reshape(depth * b_s * N_MEM, dh)

    cos_p, sin_p = _rope_tables(jnp.arange(s_len))
    cos_p, sin_p = jnp.tile(cos_p, (b_p, 1)), jnp.tile(sin_p, (b_p, 1))
    cos_s, sin_s = _rope_tables(past_len + jnp.arange(t_new))
    cos_s, sin_s = jnp.tile(cos_s, (b_s, 1)), jnp.tile(sin_s, (b_s, 1))

    xp = x_prompt.reshape(n_p, D_MODEL)
    xs = x_sample.reshape(n_s, D_MODEL)
    mem2 = mem_prompt.reshape(b_p * N_MEM, D_MODEL)
    zero_cbuf = jnp.zeros((b_p, SUBLANES, GDN_QKV), F32)
    zero_state = jnp.zeros((b_p, GDN_H, GDN_DK, LANES), F32)
    tpad = SUBLANES

    outs = {k: [] for k in ("p_ckv", "p_kr", "p_gdn", "p_conv", "p_mk", "p_mv", "s_ckv", "s_kr", "s_gdn", "s_conv")}
    for l in range(depth):
        gdn_w = (gdn_conv_w[l], gdn_a_log[l], gdn_dt_bias[l], gdn_norm_g[l])
        mla_w = (mla_q_norm_g[l], mla_kv_norm_g[l], wuq[l], wuk[l])

        def prompt_seq(proj):
            og, s_new = gdn(proj, zero_cbuf, zero_state, *gdn_w, batch=b_p, rb=CHUNK, valid=CHUNK)
            qcat, kvcat, ckv, kr = mla_prep(proj, cos_p, sin_p, *mla_w)
            olat = mla_flash(qcat, kvcat, batch=b_p, seq=s_len)
            return og, olat, (s_new, ckv, kr)

        xp, proj, (sp, ckvp, krp) = _mix_and_ffn(xp, l, wts, prompt_seq)
        outs["p_gdn"].append(sp)
        outs["p_ckv"].append(ckvp.reshape(b_p, s_len, MLA_KV_LORA))
        outs["p_kr"].append(krp.reshape(b_p, s_len, MLA_ROPE))
        qkv_pre = proj[:, :GDN_QKV].reshape(b_p, s_len, GDN_QKV)
        outs["p_conv"].append(qkv_pre[:, s_len - (GDN_CONV - 1):])
        mkv = norm_matmul(mem2, mem_norm_g[l], mem_wkv[l])
        outs["p_mk"].append(mkv[:, :dh].reshape(b_p, N_MEM, MEM_H, MEM_DH))
        outs["p_mv"].append(mkv[:, dh:].reshape(b_p, N_MEM, MEM_H, MEM_DH))
        qm = norm_matmul(xp, ln_mem_g[l], mem_wq_b[l])
        xp = mem_attn(xp, qm, mkv, mkv, mem_wo_b[l], batch=b_p, tq=256, kblk0=0, vblk0=0, vcol=1)
        xp = _peer(xp, l, wts)

        def sample_seq(proj):
            proj_pad = jnp.pad(proj.reshape(b_s, t_new, D_PROJ), ((0, 0), (0, tpad - t_new), (0, 0)))
            cbuf = jnp.pad(state_gdn_conv[l], ((0, 0), (SUBLANES - (GDN_CONV - 1), 0), (0, 0)))
            og, s_new = gdn(proj_pad.reshape(b_s * tpad, D_PROJ), cbuf, state_gdn[l], *gdn_w,
                            batch=b_s, rb=tpad, valid=t_new)
            og = og.reshape(b_s, tpad, -1)[:, :t_new].reshape(n_s, -1)
            qcat, kvcat, ckv, kr = mla_prep(proj, cos_s, sin_s, *mla_w)
            olat = mla_paged(qcat, kvcat, cache_mla_ckv, cache_mla_krope, page_table, layer=l, t_new=t_new)
            return og, olat, (s_new, ckv, kr)

        xs, proj_s, (ss, ckvs, krs) = _mix_and_ffn(xs, l, wts, sample_seq)
        outs["s_gdn"].append(ss)
        outs["s_ckv"].append(ckvs.reshape(b_s, t_new, MLA_KV_LORA))
        outs["s_kr"].append(krs.reshape(b_s, t_new, MLA_ROPE))
        qkv_pre = proj_s[:, :GDN_QKV].reshape(b_s, t_new, GDN_QKV)
        outs["s_conv"].append(qkv_pre[:, t_new - (GDN_CONV - 1):])
        qm = norm_matmul(xs, ln_mem_g[l], mem_wq_b[l])
        xs_pad = jnp.pad(xs.reshape(b_s, t_new, D_MODEL), ((0, 0), (0, tpad - t_new), (0, 0)))
        qm_pad = jnp.pad(qm.reshape(b_s, t_new, dh), ((0, 0), (0, tpad - t_new), (0, 0)))
        xs_pad = mem_attn(xs_pad.reshape(b_s * tpad, D_MODEL), qm_pad.reshape(b_s * tpad, dh), cache_k2, cache_v2,
                          mem_wo_b[l], batch=b_s, tq=tpad, kblk0=l * b_s, vblk0=l * b_s, vcol=0)
        xs = xs_pad.reshape(b_s, tpad, D_MODEL)[:, :t_new].reshape(n_s, D_MODEL)
        xs = _peer(xs, l, wts)

    y_prompt = final_norm(xp, final_norm_g).reshape(b_p, s_len, D_MODEL)
    y_sample = final_norm(xs, final_norm_g).reshape(b_s, t_new, D_MODEL)
    st = lambda k: jnp.stack(outs[k])
    return (y_prompt, y_sample, st("p_ckv"), st("p_kr"), st("p_gdn"), st("p_conv"), st("p_mk"), st("p_mv"),
            st("s_ckv"), st("s_kr"), st("s_gdn"), st("s_conv"))
```

```python
import functools
import math

import jax
import jax.numpy as jnp
from jax import lax
from jax.experimental import pallas as pl
from jax.experimental.pallas import tpu as pltpu

F32 = jnp.float32
BF16 = jnp.bfloat16

D_MODEL = 1024
GDN_H = 4
GDN_DK = 128
GDN_CONV = 4
GDN_QKV = 1536
MLA_H = 4
MLA_Q_LORA = 384
MLA_KV_LORA = 256
MLA_NOPE = 128
MLA_ROPE = 64
ROPE_THETA = 10000.0
PAGE_SIZE = 128
N_MEM = 256
MEM_H = 4
MEM_DH = 128
PEER_H = 8
PEER_N_KEYS = 128
PEER_TOPK = 16
EPS = 1e-6

LANES = 128
SUBLANES = 8
CHUNK = 128
MLA_CAT = MLA_KV_LORA + LANES
GDN_W = GDN_H * GDN_DK
OFF_CKV = GDN_QKV + GDN_W
OFF_CQ = OFF_CKV + MLA_KV_LORA
OFF_G5 = OFF_CQ + MLA_Q_LORA
D_PROJ = OFF_G5 + LANES
VMEM_LIMIT = 56 * 1024 * 1024

NT_DIMS = (((1,), (1,)), ((), ()))


def _dot(a, b):
    return jnp.dot(a.astype(BF16), b.astype(BF16), preferred_element_type=F32)


def _dot_nt(a, b):
    return lax.dot_general(a.astype(BF16), b.astype(BF16), NT_DIMS, preferred_element_type=F32)


def _sigmoid(x):
    return 1.0 / (1.0 + jnp.exp(-x))


def _rms(x, g):
    return x * lax.rsqrt(jnp.mean(x * x, axis=-1, keepdims=True) + EPS) * g


def _params(*sem):
    return pltpu.CompilerParams(dimension_semantics=sem, vmem_limit_bytes=VMEM_LIMIT)


def _norm_matmul_kernel(x_ref, g_ref, w_ref, o_ref, *h_ref):
    y = _rms(x_ref[...], g_ref[...])
    if h_ref:
        h_ref[0][...] = y
    o_ref[...] = _dot(y, w_ref[...])


def norm_matmul(x, g, w, *, emit_normed=False, tm=256):
    n, d = x.shape
    dout = w.shape[1]
    out_shape = [jax.ShapeDtypeStruct((n, dout), F32)]
    out_specs = [pl.BlockSpec((tm, dout), lambda i: (i, 0))]
    if emit_normed:
        out_shape.append(jax.ShapeDtypeStruct((n, d), F32))
        out_specs.append(pl.BlockSpec((tm, d), lambda i: (i, 0)))
    res = pl.pallas_call(
        _norm_matmul_kernel,
        grid=(n // tm,),
        in_specs=[pl.BlockSpec((tm, d), lambda i: (i, 0)),
                  pl.BlockSpec((1, d), lambda i: (0, 0)),
                  pl.BlockSpec((d, dout), lambda i: (0, 0))],
        out_specs=out_specs,
        out_shape=out_shape,
        compiler_params=_params("arbitrary"),
        name="norm_matmul",
    )(x, g.reshape(1, d), w)
    return res if emit_normed else res[0]


def _final_norm_kernel(x_ref, g_ref, o_ref):
    o_ref[...] = _rms(x_ref[...], g_ref[...])


def final_norm(x, g, tm=256):
    n, d = x.shape
    return pl.pallas_call(
        _final_norm_kernel,
        grid=(n // tm,),
        in_specs=[pl.BlockSpec((tm, d), lambda i: (i, 0)), pl.BlockSpec((1, d), lambda i: (0, 0))],
        out_specs=pl.BlockSpec((tm, d), lambda i: (i, 0)),
        out_shape=jax.ShapeDtypeStruct((n, d), F32),
        compiler_params=_params("arbitrary"),
        name="final_norm",
    )(x, g.reshape(1, d))


def _gdn_kernel(alog_ref, dtb_ref, q_ref, k_ref, v_ref, z_ref, g5_ref, cq_ref, ck_ref, cv_ref,
                wq_ref, wk_ref, wv_ref, ng_ref, s0_ref, o_ref, sout_ref,
                s_scr, carq, cark, carv, *, rb, valid, nt):
    t = pl.program_id(1)

    @pl.when(t == 0)
    def _():
        carq[...] = cq_ref[0]
        cark[...] = ck_ref[0]
        carv[...] = cv_ref[0]
        s_scr[...] = s0_ref[0]

    def conv(x_ref, car, w_ref):
        x = x_ref[...]
        cat = jnp.concatenate([car[...], x], axis=0)
        w = w_ref[...]
        y = cat[SUBLANES:SUBLANES + rb] * w[3:4]
        for i in range(1, GDN_CONV):
            y = y + cat[SUBLANES - i:SUBLANES - i + rb] * w[3 - i:4 - i]
        car[...] = x[rb - SUBLANES:rb]
        return y * _sigmoid(y)

    def l2n(x):
        return x * lax.rsqrt(jnp.sum(x * x, axis=-1, keepdims=True) + EPS)

    q_all = conv(q_ref, carq, wq_ref)
    k_all = conv(k_ref, cark, wk_ref)
    v_all = conv(v_ref, carv, wv_ref)
    g5 = g5_ref[...]
    lane = lax.broadcasted_iota(jnp.int32, g5.shape, 1)
    zz = z_ref[...]
    zgate = zz * _sigmoid(zz)

    row = lax.broadcasted_iota(jnp.int32, (CHUNK, CHUNK), 0)
    col = lax.broadcasted_iota(jnp.int32, (CHUNK, CHUNK), 1)
    causal = row >= col
    strict = row > col
    eye = (row == col).astype(F32)
    tri = causal.astype(BF16)
    n_doublings = max(int(math.ceil(math.log2(valid))) - 1, 0)
    heads = range(GDN_H)

    def split2(a):
        hi = a.astype(BF16)
        return hi, (a - hi.astype(F32)).astype(BF16)

    def dot_split(a, b):
        (ah, al), (bh, bl) = a, b
        d = functools.partial(jnp.dot, preferred_element_type=F32)
        return d(ah, bh) + (d(ah, bl) + d(al, bh))

    def prep(h):
        hs = slice(h * GDN_DK, (h + 1) * GDN_DK)
        q = l2n(q_all[:, hs]) * (GDN_DK ** -0.5)
        k = l2n(k_all[:, hs])
        v = v_all[:, hs]
        ga = jnp.sum(jnp.where(lane == MLA_ROPE + h, g5, 0.0), axis=1, keepdims=True)
        gb = jnp.sum(jnp.where(lane == MLA_ROPE + GDN_H + h, g5, 0.0), axis=1, keepdims=True)
        beta = _sigmoid(gb)
        sp = ga + dtb_ref[h]
        softplus = jnp.maximum(sp, 0.0) + jnp.log(1.0 + jnp.exp(-jnp.abs(sp)))
        g = -jnp.exp(jnp.full((rb, 1), alog_ref[h], F32)) * softplus
        if valid < rb:
            rvalid = lax.broadcasted_iota(jnp.int32, (rb, 1), 0) < valid
            beta = jnp.where(rvalid, beta, 0.0)
            g = jnp.where(rvalid, g, 0.0)
        if rb < CHUNK:
            def padr(a):
                return jnp.concatenate([a, jnp.zeros((CHUNK - rb, a.shape[1]), F32)], axis=0)
            q, k, v, beta, g = padr(q), padr(k), padr(v), padr(beta), padr(g)
        return q, k, v, beta, g

    q, k, v, beta, g = zip(*[prep(h) for h in heads])

    def cumsum_rows(gh):
        g_b = jnp.broadcast_to(gh, (CHUNK, CHUNK))
        p0 = g_b.astype(BF16)
        r1 = g_b - p0.astype(F32)
        p1 = r1.astype(BF16)
        p2 = (r1 - p1.astype(F32)).astype(BF16)
        d = functools.partial(jnp.dot, preferred_element_type=F32)
        return d(tri, p0) + (d(tri, p1) + d(tri, p2))

    gc = [cumsum_rows(g[h]) for h in heads]
    gr = [gc[h].T for h in heads]
    decay = [jnp.where(causal, jnp.exp(jnp.where(causal, gc[h] - gr[h], 0.0)), 0.0) for h in heads]
    kb = [k[h] * beta[h] for h in heads]
    low = [jnp.where(strict, _dot_nt(kb[h], k[h]) * decay[h], 0.0) for h in heads]
    m = [-low[h] for h in heads]
    tinv = [eye + m[h] for h in heads]
    for _ in range(n_doublings):
        ms = [split2(m[h]) for h in heads]
        m = [dot_split(ms[h], ms[h]) for h in heads]
        ms = [split2(m[h]) for h in heads]
        tinv = [tinv[h] + dot_split(split2(tinv[h]), ms[h]) for h in heads]
    eg = [jnp.exp(gc[h]) for h in heads]
    u = [_dot(tinv[h], v[h] * beta[h]) for h in heads]
    w = [_dot(tinv[h], kb[h] * eg[h]) for h in heads]
    qk = [_dot_nt(q[h], k[h]) * decay[h] for h in heads]
    g_last = [gc[h][CHUNK - 1:CHUNK, :] for h in heads]
    s = [s_scr[h] for h in heads]
    v_new = [u[h] - _dot(w[h], s[h]) for h in heads]
    o = [_dot(q[h] * eg[h], s[h]) + _dot(qk[h], v_new[h]) for h in heads]
    k_dec = [k[h] * jnp.exp(g_last[h] - gc[h]) for h in heads]
    for h in heads:
        s_scr[h] = s[h] * jnp.exp(g_last[h]) + _dot(k_dec[h].T, v_new[h])
        hs = slice(h * GDN_DK, (h + 1) * GDN_DK)
        o_ref[:, hs] = _rms(o[h][:rb], ng_ref[...]) * zgate[:, hs]

    @pl.when(t == nt - 1)
    def _():
        sout_ref[0] = s_scr[...]


def gdn(proj, cbuf, s0, conv_w, a_log, dt_bias, norm_g, *, batch, rb, valid):
    rows = proj.shape[0]
    nt = rows // (batch * rb)
    kern = functools.partial(_gdn_kernel, rb=rb, valid=valid, nt=nt)

    def colblk(c):
        return pl.BlockSpec((rb, GDN_W), lambda b, t, *_: (b * nt + t, c))

    def cblk(c):
        return pl.BlockSpec((1, SUBLANES, GDN_W), lambda b, t, *_: (b, 0, c))

    def wblk(c):
        return pl.BlockSpec((GDN_CONV, GDN_W), lambda b, t, *_: (0, c))

    state_spec = pl.BlockSpec((1, GDN_H, GDN_DK, LANES), lambda b, t, *_: (b, 0, 0, 0))
    grid_spec = pltpu.PrefetchScalarGridSpec(
        num_scalar_prefetch=2,
        grid=(batch, nt),
        in_specs=[colblk(0), colblk(1), colblk(2), colblk(3),
                  pl.BlockSpec((rb, LANES), lambda b, t, *_: (b * nt + t, OFF_G5 // LANES)),
                  cblk(0), cblk(1), cblk(2), wblk(0), wblk(1), wblk(2),
                  pl.BlockSpec((1, LANES), lambda b, t, *_: (0, 0)),
                  state_spec],
        out_specs=[pl.BlockSpec((rb, GDN_W), lambda b, t, *_: (b * nt + t, 0)), state_spec],
        scratch_shapes=[pltpu.VMEM((GDN_H, GDN_DK, LANES), F32)] + [pltpu.VMEM((SUBLANES, GDN_W), F32)] * 3,
    )
    return pl.pallas_call(
        kern, grid_spec=grid_spec,
        out_shape=[jax.ShapeDtypeStruct((rows, GDN_W), F32),
                   jax.ShapeDtypeStruct((batch, GDN_H, GDN_DK, LANES), F32)],
        compiler_params=_params("arbitrary", "arbitrary"),
        name="gdn",
    )(a_log, dt_bias, proj, proj, proj, proj, proj, cbuf, cbuf, cbuf, conv_w, conv_w, conv_w,
      norm_g.reshape(1, LANES), s0)


def _rope(x, cos, sin_signed):
    lane = lax.broadcasted_iota(jnp.int32, x.shape, 1)
    first_half = (lane % MLA_ROPE) < (MLA_ROPE // 2)
    rot = jnp.where(first_half, pltpu.roll(x, LANES - MLA_ROPE // 2, 1), pltpu.roll(x, MLA_ROPE // 2, 1))
    return x * cos + rot * sin_signed


def _mla_prep_kernel(ckv_in_ref, cq_in_ref, g5_ref, cos_ref, sin_ref, qg_ref, kvg_ref, wuq_ref, wuk_ref,
                     qcat_ref, kvcat_ref, ckv_ref, kr_ref):
    cos = cos_ref[...]
    sin = sin_ref[...]
    cq = _rms(cq_in_ref[...], qg_ref[...])
    ckv = _rms(ckv_in_ref[...], kvg_ref[...])
    kr = _rope(g5_ref[...], cos, sin)
    ckv_ref[...] = ckv
    kr_ref[...] = kr[:, :MLA_ROPE]
    kvcat_ref[:, :MLA_KV_LORA] = ckv.astype(BF16)
    kvcat_ref[:, MLA_KV_LORA:] = kr.astype(BF16)
    qh = _dot(cq, wuq_ref[...])
    scale = (MLA_NOPE + MLA_ROPE) ** -0.5
    for h in range(MLA_H):
        q_lat = _dot(qh[:, h * LANES:(h + 1) * LANES], wuk_ref[h])
        q_rope = _rope(qh[:, (MLA_H + h) * LANES:(MLA_H + h + 1) * LANES], cos, sin)
        qcat_ref[:, h * MLA_CAT:h * MLA_CAT + MLA_KV_LORA] = (q_lat * scale).astype(BF16)
        qcat_ref[:, h * MLA_CAT + MLA_KV_LORA:(h + 1) * MLA_CAT] = (q_rope * scale).astype(BF16)


def mla_prep(proj, cos, sin, qg, kvg, wuq, wuk, tm=256):
    n = proj.shape[0]
    return pl.pallas_call(
        _mla_prep_kernel,
        grid=(n // tm,),
        in_specs=[pl.BlockSpec((tm, MLA_KV_LORA), lambda i: (i, OFF_CKV // MLA_KV_LORA)),
                  pl.BlockSpec((tm, MLA_Q_LORA), lambda i: (i, OFF_CQ // MLA_Q_LORA)),
                  pl.BlockSpec((tm, LANES), lambda i: (i, OFF_G5 // LANES)),
                  pl.BlockSpec((tm, LANES), lambda i: (i, 0)),
                  pl.BlockSpec((tm, LANES), lambda i: (i, 0)),
                  pl.BlockSpec((1, MLA_Q_LORA), lambda i: (0, 0)),
                  pl.BlockSpec((1, MLA_KV_LORA), lambda i: (0, 0)),
                  pl.BlockSpec(wuq.shape, lambda i: (0, 0)),
                  pl.BlockSpec(wuk.shape, lambda i: (0, 0, 0))],
        out_specs=[pl.BlockSpec((tm, MLA_H * MLA_CAT), lambda i: (i, 0)),
                   pl.BlockSpec((tm, MLA_CAT), lambda i: (i, 0)),
                   pl.BlockSpec((tm, MLA_KV_LORA), lambda i: (i, 0)),
                   pl.BlockSpec((tm, MLA_ROPE), lambda i: (i, 0))],
        out_shape=[jax.ShapeDtypeStruct((n, MLA_H * MLA_CAT), BF16),
                   jax.ShapeDtypeStruct((n, MLA_CAT), BF16),
                   jax.ShapeDtypeStruct((n, MLA_KV_LORA), F32),
                   jax.ShapeDtypeStruct((n, MLA_ROPE), F32)],
        compiler_params=_params("arbitrary"),
        name="mla_prep",
    )(proj, proj, proj, cos, sin, qg.reshape(1, -1), kvg.reshape(1, -1), wuq, wuk)


def _flash_kernel(q_ref, kv_ref, o_ref, m_scr, l_scr, acc_scr, *, tq, tk, nk):
    qi = pl.program_id(1)
    kj = pl.program_id(2)

    @pl.when(kj == 0)
    def _():
        m_scr[...] = jnp.full(m_scr.shape, -jnp.inf, F32)
        l_scr[...] = jnp.zeros(l_scr.shape, F32)
        acc_scr[...] = jnp.zeros(acc_scr.shape, F32)

    @pl.when(kj * tk <= qi * tq + tq - 1)
    def _():
        kv = kv_ref[...]
        qpos = qi * tq + lax.broadcasted_iota(jnp.int32, (tq, tk), 0)
        kpos = kj * tk + lax.broadcasted_iota(jnp.int32, (tq, tk), 1)
        mask = kpos <= qpos
        heads = range(MLA_H)
        s = [jnp.where(mask, lax.dot_general(q_ref[:, h * MLA_CAT:(h + 1) * MLA_CAT], kv, NT_DIMS,
                                             preferred_element_type=F32), -jnp.inf) for h in heads]
        m_prev = [m_scr[h] for h in heads]
        m_new = [jnp.maximum(m_prev[h], jnp.max(s[h], axis=1, keepdims=True)) for h in heads]
        p = [jnp.exp(s[h] - m_new[h]) for h in heads]
        alpha = [jnp.exp(m_prev[h] - m_new[h]) for h in heads]
        pv = [jnp.dot(p[h].astype(BF16), kv[:, :MLA_KV_LORA], preferred_element_type=F32) for h in heads]
        for h in heads:
            l_scr[h] = alpha[h] * l_scr[h] + jnp.sum(p[h], axis=1, keepdims=True)
            acc_scr[h] = alpha[h] * acc_scr[h] + pv[h]
            m_scr[h] = m_new[h]

    @pl.when(kj == nk - 1)
    def _():
        for h in range(MLA_H):
            o_ref[:, h * MLA_KV_LORA:(h + 1) * MLA_KV_LORA] = acc_scr[h] / l_scr[h]


def mla_flash(qcat, kvcat, *, batch, seq, tq=256, tk=256):
    nq, nk = seq // tq, seq // tk
    kern = functools.partial(_flash_kernel, tq=tq, tk=tk, nk=nk)

    def kv_map(b, i, j):
        last = (i * tq + tq - 1) // tk
        return (b * nk + jnp.minimum(j, last), 0)

    return pl.pallas_call(
        kern,
        grid=(batch, nq, nk),
        in_specs=[pl.BlockSpec((tq, MLA_H * MLA_CAT), lambda b, i, j: (b * nq + i, 0)),
                  pl.BlockSpec((tk, MLA_CAT), kv_map)],
        out_specs=pl.BlockSpec((tq, MLA_H * MLA_KV_LORA), lambda b, i, j: (b * nq + i, 0)),
        out_shape=jax.ShapeDtypeStruct((batch * seq, MLA_H * MLA_KV_LORA), F32),
        scratch_shapes=[pltpu.VMEM((MLA_H, tq, 1), F32), pltpu.VMEM((MLA_H, tq, 1), F32),
                        pltpu.VMEM((MLA_H, tq, MLA_KV_LORA), F32)],
        compiler_params=_params("arbitrary", "arbitrary", "arbitrary"),
        name="mla_flash",
    )(qcat, kvcat)


PAGE_GROUP = 8
PAGE_SLOTS = 4
PAGE_AHEAD = 2


def _mla_paged_kernel(pt_ref, q_ref, kvn_ref, ckv_hbm, kr_hbm, o_ref, ckbuf, krbuf, sem_ck, sem_kr,
                      *, layer, t_new, n_groups, nb):
    b = pl.program_id(0)

    def issue(bb, g, slot):
        for i in range(PAGE_GROUP):
            page = pt_ref[bb, g * PAGE_GROUP + i]
            pltpu.make_async_copy(ckv_hbm.at[layer, page], ckbuf.at[slot, i], sem_ck.at[slot]).start()
            pltpu.make_async_copy(kr_hbm.at[layer, page], krbuf.at[slot, i], sem_kr.at[slot]).start()

    def wait(slot):
        pltpu.make_async_copy(ckbuf.at[slot], ckbuf.at[slot], sem_ck.at[slot]).wait()
        pltpu.make_async_copy(krbuf.at[slot], krbuf.at[slot], sem_kr.at[slot]).wait()

    @pl.when(b == 0)
    def _():
        for g in range(PAGE_AHEAD):
            issue(0, g, g)

    q = q_ref[0]
    q_lat = q[:, :MLA_KV_LORA]
    q_rope = q[:, MLA_KV_LORA:MLA_KV_LORA + MLA_ROPE]
    nrow = q.shape[0]

    def scores(slot):
        out = []
        for i in range(PAGE_GROUP):
            out.append(lax.dot_general(q_lat, ckbuf[slot, i].astype(BF16), NT_DIMS, preferred_element_type=F32)
                       + lax.dot_general(q_rope, krbuf[slot, i].astype(BF16), NT_DIMS, preferred_element_type=F32))
        return out

    def update(carry, s_list, v_list):
        m_prev, l_prev, acc = carry
        m_new = m_prev
        for s in s_list:
            m_new = jnp.maximum(m_new, jnp.max(s, axis=1, keepdims=True))
        alpha = jnp.exp(m_prev - m_new)
        l_new = alpha * l_prev
        acc = alpha * acc
        for s, vv in zip(s_list, v_list):
            p = jnp.exp(s - m_new)
            l_new = l_new + jnp.sum(p, axis=1, keepdims=True)
            acc = acc + jnp.dot(p.astype(BF16), vv, preferred_element_type=F32)
        return m_new, l_new, acc

    def values(slot):
        return [ckbuf[slot, i].astype(BF16) for i in range(PAGE_GROUP)]

    carry = (jnp.full((nrow, 1), -jnp.inf, F32), jnp.zeros((nrow, 1), F32), jnp.zeros((nrow, MLA_KV_LORA), F32))
    pending = None
    for g in range(n_groups):
        slot = g % PAGE_SLOTS
        wait(slot)
        s_list = scores(slot)
        if pending is not None:
            carry = update(carry, pending[0], values(pending[1]))
        nxt = g + PAGE_AHEAD
        if nxt < n_groups:
            issue(b, nxt, nxt % PAGE_SLOTS)
        else:
            @pl.when(b + 1 < nb)
            def _():
                issue(b + 1, nxt - n_groups, nxt % PAGE_SLOTS)
        pending = (s_list, slot)
    carry = update(carry, pending[0], values(pending[1]))

    kvn = kvn_ref[0]
    kvn = jnp.concatenate([kvn, jnp.zeros((LANES - kvn.shape[0], MLA_CAT), BF16)], axis=0)
    s = lax.dot_general(q, kvn, NT_DIMS, preferred_element_type=F32)
    tok = lax.broadcasted_iota(jnp.int32, (nrow, LANES), 0) // MLA_H
    cpos = lax.broadcasted_iota(jnp.int32, (nrow, LANES), 1)
    s = jnp.where((cpos <= tok) & (cpos < t_new), s, -jnp.inf)
    _, l_fin, acc = update(carry, [s], [kvn[:, :MLA_KV_LORA]])
    o_ref[0] = acc / l_fin


def mla_paged(qcat, kvcat_new, cache_ckv, cache_kr, page_table, *, layer, t_new):
    nb, n_pages = page_table.shape
    n_groups = n_pages // PAGE_GROUP
    assert n_groups % PAGE_SLOTS == 0
    nrow = t_new * MLA_H
    q3 = qcat.reshape(nb, nrow, MLA_CAT)
    kvn = jnp.pad(kvcat_new.reshape(nb, t_new, MLA_CAT), ((0, 0), (0, 2 * SUBLANES - t_new), (0, 0)))
    kern = functools.partial(_mla_paged_kernel, layer=layer, t_new=t_new, n_groups=n_groups, nb=nb)
    grid_spec = pltpu.PrefetchScalarGridSpec(
        num_scalar_prefetch=1,
        grid=(nb,),
        in_specs=[pl.BlockSpec((1, nrow, MLA_CAT), lambda b, pt: (b, 0, 0)),
                  pl.BlockSpec((1, 2 * SUBLANES, MLA_CAT), lambda b, pt: (b, 0, 0)),
                  pl.BlockSpec(memory_space=pl.ANY),
                  pl.BlockSpec(memory_space=pl.ANY)],
        out_specs=pl.BlockSpec((1, nrow, MLA_KV_LORA), lambda b, pt: (b, 0, 0)),
        scratch_shapes=[pltpu.VMEM((PAGE_SLOTS, PAGE_GROUP, PAGE_SIZE, MLA_KV_LORA), F32),
                        pltpu.VMEM((PAGE_SLOTS, PAGE_GROUP, PAGE_SIZE, MLA_ROPE), F32),
                        pltpu.SemaphoreType.DMA((PAGE_SLOTS,)), pltpu.SemaphoreType.DMA((PAGE_SLOTS,))],
    )
    out = pl.pallas_call(
        kern, grid_spec=grid_spec,
        out_shape=jax.ShapeDtypeStruct((nb, nrow, MLA_KV_LORA), F32),
        compiler_params=_params("arbitrary"),
        name="mla_paged",
    )(page_table, q3, kvn, cache_ckv, cache_kr)
    return out.reshape(nb * t_new, MLA_H * MLA_KV_LORA)


def _outproj_kernel(x_ref, og_ref, ol_ref, wuv_ref, wo_ref, o_ref):
    parts = [og_ref[...]]
    for h in range(MLA_H):
        parts.append(_dot(ol_ref[:, h * MLA_KV_LORA:(h + 1) * MLA_KV_LORA], wuv_ref[h]))
    mixed = jnp.concatenate(parts, axis=1)
    o_ref[...] = x_ref[...] + _dot(mixed, wo_ref[...])


def outproj(x, og, olat, wuv, wout, tm=256):
    n = x.shape[0]
    return pl.pallas_call(
        _outproj_kernel,
        grid=(n // tm,),
        in_specs=[pl.BlockSpec((tm, D_MODEL), lambda i: (i, 0)),
                  pl.BlockSpec((tm, og.shape[1]), lambda i: (i, 0)),
                  pl.BlockSpec((tm, olat.shape[1]), lambda i: (i, 0)),
                  pl.BlockSpec(wuv.shape, lambda i: (0, 0, 0)),
                  pl.BlockSpec(wout.shape, lambda i: (0, 0))],
        out_specs=pl.BlockSpec((tm, D_MODEL), lambda i: (i, 0)),
        out_shape=jax.ShapeDtypeStruct((n, D_MODEL), F32),
        compiler_params=_params("arbitrary"),
        name="outproj",
    )(x, og, olat, wuv, wout)


def _mem_attn_kernel(x_ref, q_ref, mk_ref, mv_ref, wo_ref, o_ref):
    parts = []
    scale = MEM_DH ** -0.5
    for h in range(MEM_H):
        sl = slice(h * MEM_DH, (h + 1) * MEM_DH)
        s = _dot_nt(q_ref[:, sl], mk_ref[:, sl]) * scale
        s = s - jnp.max(s, axis=1, keepdims=True)
        p = jnp.exp(s)
        p = p / jnp.sum(p, axis=1, keepdims=True)
        parts.append(_dot(p, mv_ref[:, sl]))
    o_ref[...] = x_ref[...] + _dot(jnp.concatenate(parts, axis=1), wo_ref[...])


def mem_attn(x, q, mk, mv, wo, *, batch, tq, kblk0, vblk0, vcol):
    n = x.shape[0]
    nq = n // (batch * tq)
    dh = MEM_H * MEM_DH
    return pl.pallas_call(
        _mem_attn_kernel,
        grid=(batch, nq),
        in_specs=[pl.BlockSpec((tq, D_MODEL), lambda b, i: (b * nq + i, 0)),
                  pl.BlockSpec((tq, dh), lambda b, i: (b * nq + i, 0)),
                  pl.BlockSpec((N_MEM, dh), lambda b, i: (kblk0 + b, 0)),
                  pl.BlockSpec((N_MEM, dh), lambda b, i: (vblk0 + b, vcol)),
                  pl.BlockSpec(wo.shape, lambda b, i: (0, 0))],
        out_specs=pl.BlockSpec((tq, D_MODEL), lambda b, i: (b * nq + i, 0)),
        out_shape=jax.ShapeDtypeStruct((n, D_MODEL), F32),
        compiler_params=_params("arbitrary", "arbitrary"),
        name="mem_attn",
    )(x, q, mk, mv, wo)


def _top16(s, payload=None):
    r = s.shape[0]
    iota = lax.broadcasted_iota(jnp.int32, s.shape, 0)
    vals, idxs = [], []
    for _ in range(PEER_TOPK):
        m = jnp.max(s, axis=0, keepdims=True)
        idx = jnp.min(jnp.where(s == m, iota, r), axis=0, keepdims=True)
        hit = iota == idx
        vals.append(m)
        if payload is None:
            idxs.append(idx)
        else:
            idxs.append(jnp.max(jnp.where(hit, payload, -1), axis=0, keepdims=True))
        s = jnp.where(hit, -jnp.inf, s)
    return vals, idxs


_PAIR_ROWS = [PEER_TOPK // (a + 1) for a in range(PEER_TOPK)]


def _peer_topk_kernel(q_ref, sk1_ref, sk2_ref, eid_ref, gate_ref):
    tm = q_ref.shape[0]
    row8 = lax.broadcasted_iota(jnp.int32, (SUBLANES, tm), 0)
    n_wide = sum(1 for r in _PAIR_ROWS if r > 1)
    for h in range(PEER_H):
        qh = q_ref[:, h * LANES:(h + 1) * LANES]
        s1 = _dot_nt(sk1_ref[h], qh)
        s2 = _dot_nt(sk2_ref[h], qh)
        v1, i1 = _top16(s1)
        v2, i2 = _top16(s2)
        v2a = jnp.concatenate(v2, axis=0)
        i2a = jnp.concatenate(i2, axis=0)
        cand, cidx = [], []
        for a in range(n_wide):
            nb = _PAIR_ROWS[a]
            rows = -(-nb // SUBLANES) * SUBLANES
            va = jnp.broadcast_to(v1[a], (rows, tm)) + v2a[:rows]
            ia = jnp.broadcast_to(i1[a], (rows, tm)) * PEER_N_KEYS + i2a[:rows]
            if nb < rows:
                va = jnp.where(row8 < nb, va, -jnp.inf)
            cand.append(va)
            cidx.append(ia)
        cand.append(jnp.concatenate(v1[n_wide:], axis=0) + jnp.broadcast_to(v2[0], (PEER_TOPK - n_wide, tm)))
        cidx.append(jnp.concatenate(i1[n_wide:], axis=0) * PEER_N_KEYS
                    + jnp.broadcast_to(i2[0], (PEER_TOPK - n_wide, tm)))
        sc, eid = _top16(jnp.concatenate(cand, axis=0), jnp.concatenate(cidx, axis=0))
        sc = jnp.concatenate(sc, axis=0)
        e = jnp.exp(sc - sc[0:1])
        gate_ref[h * PEER_TOPK:(h + 1) * PEER_TOPK, :] = e / jnp.sum(e, axis=0, keepdims=True)
        eid_ref[h * PEER_TOPK:(h + 1) * PEER_TOPK, :] = jnp.concatenate(eid, axis=0)


def peer_topk(q, sk1, sk2, tm=128):
    n = q.shape[0]
    rows = PEER_H * PEER_TOPK
    return pl.pallas_call(
        _peer_topk_kernel,
        grid=(n // tm,),
        in_specs=[pl.BlockSpec((tm, D_MODEL), lambda i: (i, 0)),
                  pl.BlockSpec(sk1.shape, lambda i: (0, 0, 0)),
                  pl.BlockSpec(sk2.shape, lambda i: (0, 0, 0))],
        out_specs=[pl.BlockSpec((rows, tm), lambda i: (0, i)),
                   pl.BlockSpec((rows, tm), lambda i: (0, i))],
        out_shape=[jax.ShapeDtypeStruct((rows, n), jnp.int32),
                   jax.ShapeDtypeStruct((rows, n), F32)],
        compiler_params=_params("arbitrary"),
        name="peer_topk",
    )(q, sk1, sk2)


PEER_ROWS = PEER_H * PEER_TOPK
PEER_TOK_BLOCK = 128
PEER_SLOTS = 8
PEER_CHUNKS = D_MODEL // LANES
PEER_TAB_BLOCK = 512
U32 = jnp.uint32


def _gelu_tanh(x):
    return 0.5 * x * (1.0 + jnp.tanh(math.sqrt(2.0 / math.pi) * (x + 0.044715 * x * x * x)))


def _peer_table_kernel(u_ref, v_ref, o_ref):
    eb = u_ref.shape[0]
    quarter = PEER_CHUNKS // 2

    def pack(a, b):
        wa = pltpu.bitcast(a.astype(BF16).astype(F32), U32)
        wb = pltpu.bitcast(b.astype(BF16).astype(F32), U32)
        return wa | (wb >> 16)

    for c in range(quarter):
        lo, mid, hi = 2 * c * LANES, (2 * c + 1) * LANES, (2 * c + 2) * LANES
        o_ref[pl.ds(c, eb, stride=PEER_CHUNKS), :] = pack(u_ref[:, lo:mid], u_ref[:, mid:hi])
        o_ref[pl.ds(quarter + c, eb, stride=PEER_CHUNKS), :] = pack(v_ref[:, lo:mid], v_ref[:, mid:hi])


def peer_table(u2, v2):
    rows = u2.shape[0]
    eb = PEER_TAB_BLOCK
    return pl.pallas_call(
        _peer_table_kernel,
        grid=(rows // eb,),
        in_specs=[pl.BlockSpec((eb, D_MODEL), lambda i: (i, 0)), pl.BlockSpec((eb, D_MODEL), lambda i: (i, 0))],
        out_specs=pl.BlockSpec((eb * PEER_CHUNKS, LANES), lambda i: (i, 0)),
        out_shape=jax.ShapeDtypeStruct((rows * PEER_CHUNKS, LANES), U32),
        compiler_params=_params("arbitrary"),
        name="peer_table",
    )(u2, v2)


def _peer_gather_kernel(eid_ref, gate_ref, h_ref, x_ref, tab_ref, o_ref, *scratch, nsteps):
    bufs, sem = scratch[:PEER_SLOTS], scratch[PEER_SLOTS]
    ntok = x_ref.shape[0]
    lookahead = PEER_SLOTS - 1
    step = pl.program_id(0)
    half = PEER_CHUNKS // 2

    def issue(tok, slot):
        for k in range(PEER_ROWS):
            row0 = pl.multiple_of(eid_ref[0, tok, k] * PEER_CHUNKS, PEER_CHUNKS)
            pltpu.make_async_copy(tab_ref.at[pl.ds(row0, PEER_CHUNKS), :], bufs[slot].at[:, k, :],
                                  sem.at[slot]).start(priority=k % 2)

    def wait(slot):
        pltpu.make_async_copy(bufs[slot], bufs[slot], sem.at[slot]).wait()

    lane_row = lax.broadcasted_iota(jnp.int32, (1, ntok), 1)

    def unpack(words):
        return pltpu.bitcast(words & jnp.uint32(0xFFFF0000), F32), pltpu.bitcast(words << 16, F32)

    def compute(tok, slot):
        buf = bufs[slot]
        hrow = h_ref[pl.ds(tok, 1), :]
        acc = None
        for c in range(half):
            hi, lo = unpack(buf[c])
            t = (hi * hrow[:, 2 * c * LANES:(2 * c + 1) * LANES]
                 + lo * hrow[:, (2 * c + 1) * LANES:(2 * c + 2) * LANES])
            acc = t if acc is None else acc + t
        act = jnp.sum(acc, axis=1, keepdims=True)
        gcol = jnp.sum(jnp.where(lane_row == tok, gate_ref[...], 0.0), axis=1, keepdims=True)
        w = gcol * _gelu_tanh(act)
        ys = []
        for c in range(half):
            hi, lo = unpack(buf[half + c])
            ys.append(jnp.sum(hi * w, axis=0, keepdims=True))
            ys.append(jnp.sum(lo * w, axis=0, keepdims=True))
        o_ref[pl.ds(tok, 1), :] = x_ref[pl.ds(tok, 1), :] + jnp.concatenate(ys, axis=1)

    @pl.when(step == 0)
    def _():
        for t in range(lookahead):
            issue(t, t)

    def group(g, carry):
        for j in range(PEER_SLOTS):
            tok = g * PEER_SLOTS + j
            wait(j)
            issue(tok + lookahead, (j + lookahead) % PEER_SLOTS)
            compute(tok, j)
        return carry

    lax.fori_loop(0, ntok // PEER_SLOTS, group, 0)

    @pl.when(step == nsteps - 1)
    def _():
        for t in range(lookahead):
            wait(t)


def peer_gather(eid, gate_t, hn, x, table):
    n = x.shape[0]
    tb = PEER_TOK_BLOCK
    nsteps = n // tb
    eid3 = eid.reshape(nsteps, tb, PEER_ROWS)
    head = jnp.concatenate([eid3[1:, :PEER_SLOTS], eid3[-1:, :PEER_SLOTS]], axis=0)
    eid_ext = jnp.concatenate([eid3, head], axis=1)
    return pl.pallas_call(
        functools.partial(_peer_gather_kernel, nsteps=nsteps),
        grid=(nsteps,),
        in_specs=[pl.BlockSpec((1, tb + PEER_SLOTS, PEER_ROWS), lambda i: (i, 0, 0), memory_space=pltpu.SMEM),
                  pl.BlockSpec((PEER_ROWS, tb), lambda i: (0, i)),
                  pl.BlockSpec((tb, D_MODEL), lambda i: (i, 0)),
                  pl.BlockSpec((tb, D_MODEL), lambda i: (i, 0)),
                  pl.BlockSpec(memory_space=pl.ANY)],
        out_specs=pl.BlockSpec((tb, D_MODEL), lambda i: (i, 0)),
        out_shape=jax.ShapeDtypeStruct((n, D_MODEL), F32),
        scratch_shapes=[pltpu.VMEM((PEER_CHUNKS, PEER_ROWS, LANES), U32)] * PEER_SLOTS
                       + [pltpu.SemaphoreType.DMA((PEER_SLOTS,))],
        compiler_params=_params("arbitrary"),
        name="peer_gather",
    )(eid_ext, gate_t, hn, x, table)


def _rope_tables(pos):
    half = MLA_ROPE // 2
    inv = jnp.exp(-math.log(ROPE_THETA) * jnp.arange(half, dtype=F32) / half)
    ang = pos.astype(F32)[:, None] * inv[None, :]
    cos, sin = jnp.cos(ang), jnp.sin(ang)
    zeros = jnp.zeros_like(cos)
    cos_t = jnp.concatenate([cos, cos, zeros, zeros], axis=1)
    sin_t = jnp.concatenate([-sin, sin, zeros, zeros], axis=1)
    return cos_t, sin_t


def _mix_and_ffn(x, layer, wts, seq_fn):
    proj = norm_matmul(x, wts["ln_mix_g"][layer], wts["w_in"][layer])
    og, olat, extras = seq_fn(proj)
    x = outproj(x, og, olat, wts["w_uv"][layer], wts["w_out"][layer])
    return x, proj, extras


def _peer(x, layer, wts):
    q, hn = norm_matmul(x, wts["ln_peer_g"][layer], wts["peer_wq"][layer], emit_normed=True)
    eid_t, gate_t = peer_topk(q, wts["sk1"][layer], wts["sk2"][layer])
    eid = eid_t.T + layer * wts["n_experts"]
    return peer_gather(eid, gate_t, hn, x, wts["peer_tab"])


def kernel(x_prompt, x_sample, mem_prompt, cache_mla_ckv, cache_mla_krope, cache_mem_k, cache_mem_v, state_gdn, state_gdn_conv, page_table, ln_mix_g, w_in, gdn_conv_w, gdn_a_log, gdn_dt_bias, gdn_norm_g, mla_q_norm_g, mla_w_uq, mla_kv_norm_g, mla_w_uk, mla_w_uv, w_out, ln_mem_g, mem_norm_g, mem_wq, mem_wk, mem_wv, mem_wo, ln_peer_g, peer_wq, peer_subkeys, peer_u, peer_v, final_norm_g):
    depth = w_in.shape[0]
    b_p, s_len, _ = x_prompt.shape
    b_s, t_new, _ = x_sample.shape
    past_len = page_table.shape[1] * PAGE_SIZE
    n_p, n_s = b_p * s_len, b_s * t_new

    c_z = GDN_QKV + GDN_W
    c_a, c_b, c_cq = c_z, c_z + GDN_H, c_z + 2 * GDN_H
    c_ckv = c_cq + MLA_Q_LORA
    c_kr = c_ckv + MLA_KV_LORA
    w_in_r = jnp.concatenate(
        [w_in[:, :, :c_z], w_in[:, :, c_ckv:c_kr], w_in[:, :, c_cq:c_ckv], w_in[:, :, c_kr:c_kr + MLA_ROPE],
         w_in[:, :, c_a:c_a + GDN_H], w_in[:, :, c_b:c_b + GDN_H],
         jnp.zeros((depth, D_MODEL, LANES - MLA_ROPE - 2 * GDN_H), F32)], axis=2).astype(BF16)
    assert w_in_r.shape[2] == D_PROJ
    wuq_nope = mla_w_uq[:, :, :, :MLA_NOPE].reshape(depth, MLA_Q_LORA, MLA_H * MLA_NOPE)
    wuq_rope = jnp.pad(mla_w_uq[:, :, :, MLA_NOPE:], ((0, 0), (0, 0), (0, 0), (0, LANES - MLA_ROPE)))
    wuq = jnp.concatenate([wuq_nope, wuq_rope.reshape(depth, MLA_Q_LORA, MLA_H * LANES)], axis=2).astype(BF16)
    wuk = jnp.transpose(mla_w_uk, (0, 2, 3, 1)).astype(BF16)
    wuv = jnp.transpose(mla_w_uv, (0, 2, 1, 3)).astype(BF16)
    zk = jnp.zeros((depth, PEER_H, PEER_N_KEYS, LANES // 2), F32)
    wts = {
        "ln_mix_g": ln_mix_g, "w_in": w_in_r, "w_uv": wuv, "w_out": w_out.astype(BF16),
        "ln_peer_g": ln_peer_g, "peer_wq": peer_wq.reshape(depth, D_MODEL, PEER_H * LANES).astype(BF16),
        "sk1": jnp.concatenate([peer_subkeys[:, :, 0], zk], axis=-1).astype(BF16),
        "sk2": jnp.concatenate([zk, peer_subkeys[:, :, 1]], axis=-1).astype(BF16),
        "peer_tab": peer_table(peer_u.reshape(-1, D_MODEL), peer_v.reshape(-1, D_MODEL)),
        "n_experts": peer_u.shape[1],
    }
    mem_wkv = jnp.concatenate([mem_wk, mem_wv], axis=2).astype(BF16)
    mem_wq_b = mem_wq.astype(BF16)
    mem_wo_b = mem_wo.astype(BF16)
    dh = MEM_H * MEM_DH
    cache_k2 = cache_mem_k.reshape(depth * b_s * N_MEM, dh)
    cache_v2 = cache_mem_v.reshape(depth * b_s * N_MEM, dh)

    cos_p, sin_p = _rope_tables(jnp.arange(s_len))
    cos_p, sin_p = jnp.tile(cos_p, (b_p, 1)), jnp.tile(sin_p, (b_p, 1))
    cos_s, sin_s = _rope_tables(past_len + jnp.arange(t_new))
    cos_s, sin_s = jnp.tile(cos_s, (b_s, 1)), jnp.tile(sin_s, (b_s, 1))

    xp = x_prompt.reshape(n_p, D_MODEL)
    xs = x_sample.reshape(n_s, D_MODEL)
    mem2 = mem_prompt.reshape(b_p * N_MEM, D_MODEL)
    zero_cbuf = jnp.zeros((b_p, SUBLANES, GDN_QKV), F32)
    zero_state = jnp.zeros((b_p, GDN_H, GDN_DK, LANES), F32)
    tpad = SUBLANES

    outs = {k: [] for k in ("p_ckv", "p_kr", "p_gdn", "p_conv", "p_mk", "p_mv", "s_ckv", "s_kr", "s_gdn", "s_conv")}
    for l in range(depth):
        gdn_w = (gdn_conv_w[l], gdn_a_log[l], gdn_dt_bias[l], gdn_norm_g[l])
        mla_w = (mla_q_norm_g[l], mla_kv_norm_g[l], wuq[l], wuk[l])

        def prompt_seq(proj):
            og, s_new = gdn(proj, zero_cbuf, zero_state, *gdn_w, batch=b_p, rb=CHUNK, valid=CHUNK)
            qcat, kvcat, ckv, kr = mla_prep(proj, cos_p, sin_p, *mla_w)
            olat = mla_flash(qcat, kvcat, batch=b_p, seq=s_len)
            return og, olat, (s_new, ckv, kr)

        xp, proj, (sp, ckvp, krp) = _mix_and_ffn(xp, l, wts, prompt_seq)
        outs["p_gdn"].append(sp)
        outs["p_ckv"].append(ckvp.reshape(b_p, s_len, MLA_KV_LORA))
        outs["p_kr"].append(krp.reshape(b_p, s_len, MLA_ROPE))
        qkv_pre = proj[:, :GDN_QKV].reshape(b_p, s_len, GDN_QKV)
        outs["p_conv"].append(qkv_pre[:, s_len - (GDN_CONV - 1):])
        mkv = norm_matmul(mem2, mem_norm_g[l], mem_wkv[l])
        outs["p_mk"].append(mkv[:, :dh].reshape(b_p, N_MEM, MEM_H, MEM_DH))
        outs["p_mv"].append(mkv[:, dh:].reshape(b_p, N_MEM, MEM_H, MEM_DH))
        qm = norm_matmul(xp, ln_mem_g[l], mem_wq_b[l])
        xp = mem_attn(xp, qm, mkv, mkv, mem_wo_b[l], batch=b_p, tq=256, kblk0=0, vblk0=0, vcol=1)
        xp = _peer(xp, l, wts)

        def sample_seq(proj):
            proj_pad = jnp.pad(proj.reshape(b_s, t_new, D_PROJ), ((0, 0), (0, tpad - t_new), (0, 0)))
            cbuf = jnp.pad(state_gdn_conv[l], ((0, 0), (SUBLANES - (GDN_CONV - 1), 0), (0, 0)))
            og, s_new = gdn(proj_pad.reshape(b_s * tpad, D_PROJ), cbuf, state_gdn[l], *gdn_w,
                            batch=b_s, rb=tpad, valid=t_new)
            og = og.reshape(b_s, tpad, -1)[:, :t_new].reshape(n_s, -1)
            qcat, kvcat, ckv, kr = mla_prep(proj, cos_s, sin_s, *mla_w)
            olat = mla_paged(qcat, kvcat, cache_mla_ckv, cache_mla_krope, page_table, layer=l, t_new=t_new)
            return og, olat, (s_new, ckv, kr)

        xs, proj_s, (ss, ckvs, krs) = _mix_and_ffn(xs, l, wts, sample_seq)
        outs["s_gdn"].append(ss)
        outs["s_ckv"].append(ckvs.reshape(b_s, t_new, MLA_KV_LORA))
        outs["s_kr"].append(krs.reshape(b_s, t_new, MLA_ROPE))
        qkv_pre = proj_s[:, :GDN_QKV].reshape(b_s, t_new, GDN_QKV)
        outs["s_conv"].append(qkv_pre[:, t_new - (GDN_CONV - 1):])
        qm = norm_matmul(xs, ln_mem_g[l], mem_wq_b[l])
        xs_pad = jnp.pad(xs.reshape(b_s, t_new, D_MODEL), ((0, 0), (0, tpad - t_new), (0, 0)))
        qm_pad = jnp.pad(qm.reshape(b_s, t_new, dh), ((0, 0), (0, tpad - t_new), (0, 0)))
        xs_pad = mem_attn(xs_pad.reshape(b_s * tpad, D_MODEL), qm_pad.reshape(b_s * tpad, dh), cache_k2, cache_v2,
                          mem_wo_b[l], batch=b_s, tq=tpad, kblk0=l * b_s, vblk0=l * b_s, vcol=0)
        xs = xs_pad.reshape(b_s, tpad, D_MODEL)[:, :t_new].reshape(n_s, D_MODEL)
        xs = _peer(xs, l, wts)

    y_prompt = final_norm(xp, final_norm_g).reshape(b_p, s_len, D_MODEL)
    y_sample = final_norm(xs, final_norm_g).reshape(b_s, t_new, D_MODEL)
    st = lambda k: jnp.stack(outs[k])
    return (y_prompt, y_sample, st("p_ckv"), st("p_kr"), st("p_gdn"), st("p_conv"), st("p_mk"), st("p_mv"),
            st("s_ckv"), st("s_kr"), st("s_gdn"), st("s_conv"))
```

```python
import functools
import math

import jax
import jax.numpy as jnp
from jax import lax
from jax.experimental import pallas as pl
from jax.experimental.pallas import tpu as pltpu

F32 = jnp.float32
BF16 = jnp.bfloat16

D_MODEL = 1024
GDN_H = 4
GDN_DK = 128
GDN_CONV = 4
GDN_QKV = 1536
MLA_H = 4
MLA_Q_LORA = 384
MLA_KV_LORA = 256
MLA_NOPE = 128
MLA_ROPE = 64
ROPE_THETA = 10000.0
PAGE_SIZE = 128
N_MEM = 256
MEM_H = 4
MEM_DH = 128
PEER_H = 8
PEER_N_KEYS = 128
PEER_TOPK = 16
EPS = 1e-6

LANES = 128
SUBLANES = 8
CHUNK = 128
MLA_CAT = MLA_KV_LORA + LANES
GDN_W = GDN_H * GDN_DK
OFF_CKV = GDN_QKV + GDN_W
OFF_CQ = OFF_CKV + MLA_KV_LORA
OFF_G5 = OFF_CQ + MLA_Q_LORA
D_PROJ = OFF_G5 + LANES
VMEM_LIMIT = 56 * 1024 * 1024

NT_DIMS = (((1,), (1,)), ((), ()))


def _dot(a, b):
    return jnp.dot(a.astype(BF16), b.astype(BF16), preferred_element_type=F32)


def _dot_nt(a, b):
    return lax.dot_general(a.astype(BF16), b.astype(BF16), NT_DIMS, preferred_element_type=F32)


def _sigmoid(x):
    return 1.0 / (1.0 + jnp.exp(-x))


def _rms(x, g):
    return x * lax.rsqrt(jnp.mean(x * x, axis=-1, keepdims=True) + EPS) * g


def _params(*sem):
    return pltpu.CompilerParams(dimension_semantics=sem, vmem_limit_bytes=VMEM_LIMIT)


def _norm_matmul_kernel(x_ref, g_ref, w_ref, o_ref, *h_ref):
    y = _rms(x_ref[...], g_ref[...])
    if h_ref:
        h_ref[0][...] = y
    o_ref[...] = _dot(y, w_ref[...])


def norm_matmul(x, g, w, *, emit_normed=False, tm=256):
    n, d = x.shape
    dout = w.shape[1]
    out_shape = [jax.ShapeDtypeStruct((n, dout), F32)]
    out_specs = [pl.BlockSpec((tm, dout), lambda i: (i, 0))]
    if emit_normed:
        out_shape.append(jax.ShapeDtypeStruct((n, d), F32))
        out_specs.append(pl.BlockSpec((tm, d), lambda i: (i, 0)))
    res = pl.pallas_call(
        _norm_matmul_kernel,
        grid=(n // tm,),
        in_specs=[pl.BlockSpec((tm, d), lambda i: (i, 0)),
                  pl.BlockSpec((1, d), lambda i: (0, 0)),
                  pl.BlockSpec((d, dout), lambda i: (0, 0))],
        out_specs=out_specs,
        out_shape=out_shape,
        compiler_params=_params("arbitrary"),
        name="norm_matmul",
    )(x, g.reshape(1, d), w)
    return res if emit_normed else res[0]


def _final_norm_kernel(x_ref, g_ref, o_ref):
    o_ref[...] = _rms(x_ref[...], g_ref[...])


def final_norm(x, g, tm=256):
    n, d = x.shape
    return pl.pallas_call(
        _final_norm_kernel,
        grid=(n // tm,),
        in_specs=[pl.BlockSpec((tm, d), lambda i: (i, 0)), pl.BlockSpec((1, d), lambda i: (0, 0))],
        out_specs=pl.BlockSpec((tm, d), lambda i: (i, 0)),
        out_shape=jax.ShapeDtypeStruct((n, d), F32),
        compiler_params=_params("arbitrary"),
        name="final_norm",
    )(x, g.reshape(1, d))


def _gdn_kernel(alog_ref, dtb_ref, q_ref, k_ref, v_ref, z_ref, g5_ref, cq_ref, ck_ref, cv_ref,
                wq_ref, wk_ref, wv_ref, ng_ref, s0_ref, o_ref, sout_ref,
                s_scr, carq, cark, carv, *, rb, valid, nt):
    t = pl.program_id(1)

    @pl.when(t == 0)
    def _():
        carq[...] = cq_ref[0]
        cark[...] = ck_ref[0]
        carv[...] = cv_ref[0]
        s_scr[...] = s0_ref[0]

    def conv(x_ref, car, w_ref):
        x = x_ref[...]
        cat = jnp.concatenate([car[...], x], axis=0)
        w = w_ref[...]
        y = cat[SUBLANES:SUBLANES + rb] * w[3:4]
        for i in range(1, GDN_CONV):
            y = y + cat[SUBLANES - i:SUBLANES - i + rb] * w[3 - i:4 - i]
        car[...] = x[rb - SUBLANES:rb]
        return y * _sigmoid(y)

    def l2n(x):
        return x * lax.rsqrt(jnp.sum(x * x, axis=-1, keepdims=True) + EPS)

    q_all = conv(q_ref, carq, wq_ref)
    k_all = conv(k_ref, cark, wk_ref)
    v_all = conv(v_ref, carv, wv_ref)
    g5 = g5_ref[...]
    lane = lax.broadcasted_iota(jnp.int32, g5.shape, 1)
    zz = z_ref[...]
    zgate = zz * _sigmoid(zz)

    row = lax.broadcasted_iota(jnp.int32, (CHUNK, CHUNK), 0)
    col = lax.broadcasted_iota(jnp.int32, (CHUNK, CHUNK), 1)
    causal = row >= col
    strict = row > col
    eye = (row == col).astype(F32)
    tri = causal.astype(BF16)
    n_doublings = max(int(math.ceil(math.log2(valid))) - 1, 0)
    heads = range(GDN_H)

    def split2(a):
        hi = a.astype(BF16)
        return hi, (a - hi.astype(F32)).astype(BF16)

    def dot_split(a, b):
        (ah, al), (bh, bl) = a, b
        d = functools.partial(jnp.dot, preferred_element_type=F32)
        return d(ah, bh) + (d(ah, bl) + d(al, bh))

    def prep(h):
        hs = slice(h * GDN_DK, (h + 1) * GDN_DK)
        q = l2n(q_all[:, hs]) * (GDN_DK ** -0.5)
        k = l2n(k_all[:, hs])
        v = v_all[:, hs]
        ga = jnp.sum(jnp.where(lane == MLA_ROPE + h, g5, 0.0), axis=1, keepdims=True)
        gb = jnp.sum(jnp.where(lane == MLA_ROPE + GDN_H + h, g5, 0.0), axis=1, keepdims=True)
        beta = _sigmoid(gb)
        sp = ga + dtb_ref[h]
        softplus = jnp.maximum(sp, 0.0) + jnp.log(1.0 + jnp.exp(-jnp.abs(sp)))
        g = -jnp.exp(jnp.full((rb, 1), alog_ref[h], F32)) * softplus
        if valid < rb:
            rvalid = lax.broadcasted_iota(jnp.int32, (rb, 1), 0) < valid
            beta = jnp.where(rvalid, beta, 0.0)
            g = jnp.where(rvalid, g, 0.0)
        if rb < CHUNK:
            def padr(a):
                return jnp.concatenate([a, jnp.zeros((CHUNK - rb, a.shape[1]), F32)], axis=0)
            q, k, v, beta, g = padr(q), padr(k), padr(v), padr(beta), padr(g)
        return q, k, v, beta, g

    q, k, v, beta, g = zip(*[prep(h) for h in heads])

    def cumsum_rows(gh):
        g_b = jnp.broadcast_to(gh, (CHUNK, CHUNK))
        p0 = g_b.astype(BF16)
        r1 = g_b - p0.astype(F32)
        p1 = r1.astype(BF16)
        p2 = (r1 - p1.astype(F32)).astype(BF16)
        d = functools.partial(jnp.dot, preferred_element_type=F32)
        return d(tri, p0) + (d(tri, p1) + d(tri, p2))

    gc = [cumsum_rows(g[h]) for h in heads]
    gr = [gc[h].T for h in heads]
    decay = [jnp.where(causal, jnp.exp(jnp.where(causal, gc[h] - gr[h], 0.0)), 0.0) for h in heads]
    kb = [k[h] * beta[h] for h in heads]
    low = [jnp.where(strict, _dot_nt(kb[h], k[h]) * decay[h], 0.0) for h in heads]
    m = [-low[h] for h in heads]
    tinv = [eye + m[h] for h in heads]
    for _ in range(n_doublings):
        ms = [split2(m[h]) for h in heads]
        m = [dot_split(ms[h], ms[h]) for h in heads]
        ms = [split2(m[h]) for h in heads]
        tinv = [tinv[h] + dot_split(split2(tinv[h]), ms[h]) for h in heads]
    eg = [jnp.exp(gc[h]) for h in heads]
    u = [_dot(tinv[h], v[h] * beta[h]) for h in heads]
    w = [_dot(tinv[h], kb[h] * eg[h]) for h in heads]
    qk = [_dot_nt(q[h], k[h]) * decay[h] for h in heads]
    g_last = [gc[h][CHUNK - 1:CHUNK, :] for h in heads]
    s = [s_scr[h] for h in heads]
    v_new = [u[h] - _dot(w[h], s[h]) for h in heads]
    o = [_dot(q[h] * eg[h], s[h]) + _dot(qk[h], v_new[h]) for h in heads]
    k_dec = [k[h] * jnp.exp(g_last[h] - gc[h]) for h in heads]
    for h in heads:
        s_scr[h] = s[h] * jnp.exp(g_last[h]) + _dot(k_dec[h].T, v_new[h])
        hs = slice(h * GDN_DK, (h + 1) * GDN_DK)
        o_ref[:, hs] = _rms(o[h][:rb], ng_ref[...]) * zgate[:, hs]

    @pl.when(t == nt - 1)
    def _():
        sout_ref[0] = s_scr[...]


def gdn(proj, cbuf, s0, conv_w, a_log, dt_bias, norm_g, *, batch, rb, valid):
    rows = proj.shape[0]
    nt = rows // (batch * rb)
    kern = functools.partial(_gdn_kernel, rb=rb, valid=valid, nt=nt)

    def colblk(c):
        return pl.BlockSpec((rb, GDN_W), lambda b, t, *_: (b * nt + t, c))

    def cblk(c):
        return pl.BlockSpec((1, SUBLANES, GDN_W), lambda b, t, *_: (b, 0, c))

    def wblk(c):
        return pl.BlockSpec((GDN_CONV, GDN_W), lambda b, t, *_: (0, c))

    state_spec = pl.BlockSpec((1, GDN_H, GDN_DK, LANES), lambda b, t, *_: (b, 0, 0, 0))
    grid_spec = pltpu.PrefetchScalarGridSpec(
        num_scalar_prefetch=2,
        grid=(batch, nt),
        in_specs=[colblk(0), colblk(1), colblk(2), colblk(3),
                  pl.BlockSpec((rb, LANES), lambda b, t, *_: (b * nt + t, OFF_G5 // LANES)),
                  cblk(0), cblk(1), cblk(2), wblk(0), wblk(1), wblk(2),
                  pl.BlockSpec((1, LANES), lambda b, t, *_: (0, 0)),
                  state_spec],
        out_specs=[pl.BlockSpec((rb, GDN_W), lambda b, t, *_: (b * nt + t, 0)), state_spec],
        scratch_shapes=[pltpu.VMEM((GDN_H, GDN_DK, LANES), F32)] + [pltpu.VMEM((SUBLANES, GDN_W), F32)] * 3,
    )
    return pl.pallas_call(
        kern, grid_spec=grid_spec,
        out_shape=[jax.ShapeDtypeStruct((rows, GDN_W), F32),
                   jax.ShapeDtypeStruct((batch, GDN_H, GDN_DK, LANES), F32)],
        compiler_params=_params("arbitrary", "arbitrary"),
        name="gdn",
    )(a_log, dt_bias, proj, proj, proj, proj, proj, cbuf, cbuf, cbuf, conv_w, conv_w, conv_w,
      norm_g.reshape(1, LANES), s0)


def _rope(x, cos, sin_signed):
    lane = lax.broadcasted_iota(jnp.int32, x.shape, 1)
    first_half = (lane % MLA_ROPE) < (MLA_ROPE // 2)
    rot = jnp.where(first_half, pltpu.roll(x, LANES - MLA_ROPE // 2, 1), pltpu.roll(x, MLA_ROPE // 2, 1))
    return x * cos + rot * sin_signed


def _mla_prep_kernel(ckv_in_ref, cq_in_ref, g5_ref, cos_ref, sin_ref, qg_ref, kvg_ref, wuq_ref, wuk_ref,
                     qcat_ref, kvcat_ref, ckv_ref, kr_ref):
    cos = cos_ref[...]
    sin = sin_ref[...]
    cq = _rms(cq_in_ref[...], qg_ref[...])
    ckv = _rms(ckv_in_ref[...], kvg_ref[...])
    kr = _rope(g5_ref[...], cos, sin)
    ckv_ref[...] = ckv
    kr_ref[...] = kr[:, :MLA_ROPE]
    kvcat_ref[:, :MLA_KV_LORA] = ckv.astype(BF16)
    kvcat_ref[:, MLA_KV_LORA:] = kr.astype(BF16)
    qh = _dot(cq, wuq_ref[...])
    scale = (MLA_NOPE + MLA_ROPE) ** -0.5
    for h in range(MLA_H):
        q_lat = _dot(qh[:, h * LANES:(h + 1) * LANES], wuk_ref[h])
        q_rope = _rope(qh[:, (MLA_H + h) * LANES:(MLA_H + h + 1) * LANES], cos, sin)
        qcat_ref[:, h * MLA_CAT:h * MLA_CAT + MLA_KV_LORA] = (q_lat * scale).astype(BF16)
        qcat_ref[:, h * MLA_CAT + MLA_KV_LORA:(h + 1) * MLA_CAT] = (q_rope * scale).astype(BF16)


def mla_prep(proj, cos, sin, qg, kvg, wuq, wuk, tm=256):
    n = proj.shape[0]
    return pl.pallas_call(
        _mla_prep_kernel,
        grid=(n // tm,),
        in_specs=[pl.BlockSpec((tm, MLA_KV_LORA), lambda i: (i, OFF_CKV // MLA_KV_LORA)),
                  pl.BlockSpec((tm, MLA_Q_LORA), lambda i: (i, OFF_CQ // MLA_Q_LORA)),
                  pl.BlockSpec((tm, LANES), lambda i: (i, OFF_G5 // LANES)),
                  pl.BlockSpec((tm, LANES), lambda i: (i, 0)),
                  pl.BlockSpec((tm, LANES), lambda i: (i, 0)),
                  pl.BlockSpec((1, MLA_Q_LORA), lambda i: (0, 0)),
                  pl.BlockSpec((1, MLA_KV_LORA), lambda i: (0, 0)),
                  pl.BlockSpec(wuq.shape, lambda i: (0, 0)),
                  pl.BlockSpec(wuk.shape, lambda i: (0, 0, 0))],
        out_specs=[pl.BlockSpec((tm, MLA_H * MLA_CAT), lambda i: (i, 0)),
                   pl.BlockSpec((tm, MLA_CAT), lambda i: (i, 0)),
                   pl.BlockSpec((tm, MLA_KV_LORA), lambda i: (i, 0)),
                   pl.BlockSpec((tm, MLA_ROPE), lambda i: (i, 0))],
        out_shape=[jax.ShapeDtypeStruct((n, MLA_H * MLA_CAT), BF16),
                   jax.ShapeDtypeStruct((n, MLA_CAT), BF16),
                   jax.ShapeDtypeStruct((n, MLA_KV_LORA), F32),
                   jax.ShapeDtypeStruct((n, MLA_ROPE), F32)],
        compiler_params=_params("arbitrary"),
        name="mla_prep",
    )(proj, proj, proj, cos, sin, qg.reshape(1, -1), kvg.reshape(1, -1), wuq, wuk)


def _flash_kernel(q_ref, kv_ref, o_ref, m_scr, l_scr, acc_scr, *, tq, tk, nk):
    qi = pl.program_id(1)
    kj = pl.program_id(2)

    @pl.when(kj == 0)
    def _():
        m_scr[...] = jnp.full(m_scr.shape, -jnp.inf, F32)
        l_scr[...] = jnp.zeros(l_scr.shape, F32)
        acc_scr[...] = jnp.zeros(acc_scr.shape, F32)

    @pl.when(kj * tk <= qi * tq + tq - 1)
    def _():
        kv = kv_ref[...]
        qpos = qi * tq + lax.broadcasted_iota(jnp.int32, (tq, tk), 0)
        kpos = kj * tk + lax.broadcasted_iota(jnp.int32, (tq, tk), 1)
        mask = kpos <= qpos
        heads = range(MLA_H)
        s = [jnp.where(mask, lax.dot_general(q_ref[:, h * MLA_CAT:(h + 1) * MLA_CAT], kv, NT_DIMS,
                                             preferred_element_type=F32), -jnp.inf) for h in heads]
        m_prev = [m_scr[h] for h in heads]
        m_new = [jnp.maximum(m_prev[h], jnp.max(s[h], axis=1, keepdims=True)) for h in heads]
        p = [jnp.exp(s[h] - m_new[h]) for h in heads]
        alpha = [jnp.exp(m_prev[h] - m_new[h]) for h in heads]
        pv = [jnp.dot(p[h].astype(BF16), kv[:, :MLA_KV_LORA], preferred_element_type=F32) for h in heads]
        for h in heads:
            l_scr[h] = alpha[h] * l_scr[h] + jnp.sum(p[h], axis=1, keepdims=True)
            acc_scr[h] = alpha[h] * acc_scr[h] + pv[h]
            m_scr[h] = m_new[h]

    @pl.when(kj == nk - 1)
    def _():
        for h in range(MLA_H):
            o_ref[:, h * MLA_KV_LORA:(h + 1) * MLA_KV_LORA] = acc_scr[h] / l_scr[h]


def mla_flash(qcat, kvcat, *, batch, seq, tq=256, tk=256):
    nq, nk = seq // tq, seq // tk
    kern = functools.partial(_flash_kernel, tq=tq, tk=tk, nk=nk)

    def kv_map(b, i, j):
        last = (i * tq + tq - 1) // tk
        return (b * nk + jnp.minimum(j, last), 0)

    return pl.pallas_call(
        kern,
        grid=(batch, nq, nk),
        in_specs=[pl.BlockSpec((tq, MLA_H * MLA_CAT), lambda b, i, j: (b * nq + i, 0)),
                  pl.BlockSpec((tk, MLA_CAT), kv_map)],
        out_specs=pl.BlockSpec((tq, MLA_H * MLA_KV_LORA), lambda b, i, j: (b * nq + i, 0)),
        out_shape=jax.ShapeDtypeStruct((batch * seq, MLA_H * MLA_KV_LORA), F32),
        scratch_shapes=[pltpu.VMEM((MLA_H, tq, 1), F32), pltpu.VMEM((MLA_H, tq, 1), F32),
                        pltpu.VMEM((MLA_H, tq, MLA_KV_LORA), F32)],
        compiler_params=_params("arbitrary", "arbitrary", "arbitrary"),
        name="mla_flash",
    )(qcat, kvcat)


PAGE_GROUP = 8
PAGE_SLOTS = 4
PAGE_AHEAD = 3


def _mla_paged_kernel(pt_ref, q_ref, kvn_ref, ckv_hbm, kr_hbm, o_ref, ckbuf, krbuf, sem_ck, sem_kr,
                      *, layer, t_new, n_groups, nb):
    b = pl.program_id(0)

    def issue(bb, g, slot):
        for i in range(PAGE_GROUP):
            page = pt_ref[bb, g * PAGE_GROUP + i]
            pltpu.make_async_copy(ckv_hbm.at[layer, page], ckbuf.at[slot, i], sem_ck.at[slot]).start()
            pltpu.make_async_copy(kr_hbm.at[layer, page], krbuf.at[slot, i], sem_kr.at[slot]).start()

    def wait(slot):
        pltpu.make_async_copy(ckbuf.at[slot], ckbuf.at[slot], sem_ck.at[slot]).wait()
        pltpu.make_async_copy(krbuf.at[slot], krbuf.at[slot], sem_kr.at[slot]).wait()

    @pl.when(b == 0)
    def _():
        for g in range(PAGE_AHEAD):
            issue(0, g, g)

    q = q_ref[0]
    q_lat = q[:, :MLA_KV_LORA]
    q_rope = q[:, MLA_KV_LORA:MLA_KV_LORA + MLA_ROPE]
    nrow = q.shape[0]

    def scores(slot):
        out = []
        for i in range(PAGE_GROUP):
            out.append(lax.dot_general(q_lat, ckbuf[slot, i].astype(BF16), NT_DIMS, preferred_element_type=F32)
                       + jnp.dot(q_rope, krbuf[slot, i].astype(BF16), preferred_element_type=F32))
        return out

    def update(carry, s_list, v_list):
        m_prev, l_prev, acc = carry
        m_new = m_prev
        for s in s_list:
            m_new = jnp.maximum(m_new, jnp.max(s, axis=1, keepdims=True))
        alpha = jnp.exp(m_prev - m_new)
        l_new = alpha * l_prev
        acc = alpha * acc
        for s, vv in zip(s_list, v_list):
            p = jnp.exp(s - m_new)
            l_new = l_new + jnp.sum(p, axis=1, keepdims=True)
            acc = acc + jnp.dot(p.astype(BF16), vv, preferred_element_type=F32)
        return m_new, l_new, acc

    def values(slot):
        return [ckbuf[slot, i].astype(BF16) for i in range(PAGE_GROUP)]

    carry = (jnp.full((nrow, 1), -jnp.inf, F32), jnp.zeros((nrow, 1), F32), jnp.zeros((nrow, MLA_KV_LORA), F32))
    pending = None
    for g in range(n_groups):
        slot = g % PAGE_SLOTS
        wait(slot)
        s_list = scores(slot)
        if pending is not None:
            carry = update(carry, pending[0], values(pending[1]))
        nxt = g + PAGE_AHEAD
        if nxt < n_groups:
            issue(b, nxt, nxt % PAGE_SLOTS)
        else:
            @pl.when(b + 1 < nb)
            def _():
                issue(b + 1, nxt - n_groups, nxt % PAGE_SLOTS)
        pending = (s_list, slot)
    carry = update(carry, pending[0], values(pending[1]))

    kvn = kvn_ref[0]
    kvn = jnp.concatenate([kvn, jnp.zeros((LANES - kvn.shape[0], MLA_CAT), BF16)], axis=0)
    s = lax.dot_general(q, kvn, NT_DIMS, preferred_element_type=F32)
    tok = lax.broadcasted_iota(jnp.int32, (nrow, LANES), 0) // MLA_H
    cpos = lax.broadcasted_iota(jnp.int32, (nrow, LANES), 1)
    s = jnp.where((cpos <= tok) & (cpos < t_new), s, -jnp.inf)
    _, l_fin, acc = update(carry, [s], [kvn[:, :MLA_KV_LORA]])
    o_ref[0] = acc / l_fin


def mla_paged(qcat, kvcat_new, cache_ckv, cache_kr, page_table, *, layer, t_new):
    nb, n_pages = page_table.shape
    n_groups = n_pages // PAGE_GROUP
    assert n_groups % PAGE_SLOTS == 0
    nrow = t_new * MLA_H
    q3 = qcat.reshape(nb, nrow, MLA_CAT)
    kvn = jnp.pad(kvcat_new.reshape(nb, t_new, MLA_CAT), ((0, 0), (0, 2 * SUBLANES - t_new), (0, 0)))
    kern = functools.partial(_mla_paged_kernel, layer=layer, t_new=t_new, n_groups=n_groups, nb=nb)
    grid_spec = pltpu.PrefetchScalarGridSpec(
        num_scalar_prefetch=1,
        grid=(nb,),
        in_specs=[pl.BlockSpec((1, nrow, MLA_CAT), lambda b, pt: (b, 0, 0)),
                  pl.BlockSpec((1, 2 * SUBLANES, MLA_CAT), lambda b, pt: (b, 0, 0)),
                  pl.BlockSpec(memory_space=pl.ANY),
                  pl.BlockSpec(memory_space=pl.ANY)],
        out_specs=pl.BlockSpec((1, nrow, MLA_KV_LORA), lambda b, pt: (b, 0, 0)),
        scratch_shapes=[pltpu.VMEM((PAGE_SLOTS, PAGE_GROUP, PAGE_SIZE, MLA_KV_LORA), F32),
                        pltpu.VMEM((PAGE_SLOTS, PAGE_GROUP, MLA_ROPE, PAGE_SIZE), F32),
                        pltpu.SemaphoreType.DMA((PAGE_SLOTS,)), pltpu.SemaphoreType.DMA((PAGE_SLOTS,))],
    )
    out = pl.pallas_call(
        kern, grid_spec=grid_spec,
        out_shape=jax.ShapeDtypeStruct((nb, nrow, MLA_KV_LORA), F32),
        compiler_params=_params("arbitrary"),
        name="mla_paged",
    )(page_table, q3, kvn, cache_ckv, cache_kr)
    return out.reshape(nb * t_new, MLA_H * MLA_KV_LORA)


def _outproj_kernel(x_ref, og_ref, ol_ref, wuv_ref, wo_ref, o_ref):
    parts = [og_ref[...]]
    for h in range(MLA_H):
        parts.append(_dot(ol_ref[:, h * MLA_KV_LORA:(h + 1) * MLA_KV_LORA], wuv_ref[h]))
    mixed = jnp.concatenate(parts, axis=1)
    o_ref[...] = x_ref[...] + _dot(mixed, wo_ref[...])


def outproj(x, og, olat, wuv, wout, tm=256):
    n = x.shape[0]
    return pl.pallas_call(
        _outproj_kernel,
        grid=(n // tm,),
        in_specs=[pl.BlockSpec((tm, D_MODEL), lambda i: (i, 0)),
                  pl.BlockSpec((tm, og.shape[1]), lambda i: (i, 0)),
                  pl.BlockSpec((tm, olat.shape[1]), lambda i: (i, 0)),
                  pl.BlockSpec(wuv.shape, lambda i: (0, 0, 0)),
                  pl.BlockSpec(wout.shape, lambda i: (0, 0))],
        out_specs=pl.BlockSpec((tm, D_MODEL), lambda i: (i, 0)),
        out_shape=jax.ShapeDtypeStruct((n, D_MODEL), F32),
        compiler_params=_params("arbitrary"),
        name="outproj",
    )(x, og, olat, wuv, wout)


def _mem_attn_kernel(x_ref, q_ref, mk_ref, mv_ref, wo_ref, o_ref):
    parts = []
    scale = MEM_DH ** -0.5
    for h in range(MEM_H):
        sl = slice(h * MEM_DH, (h + 1) * MEM_DH)
        s = _dot_nt(q_ref[:, sl], mk_ref[:, sl]) * scale
        s = s - jnp.max(s, axis=1, keepdims=True)
        p = jnp.exp(s)
        p = p / jnp.sum(p, axis=1, keepdims=True)
        parts.append(_dot(p, mv_ref[:, sl]))
    o_ref[...] = x_ref[...] + _dot(jnp.concatenate(parts, axis=1), wo_ref[...])


def mem_attn(x, q, mk, mv, wo, *, batch, tq, kblk0, vblk0, vcol):
    n = x.shape[0]
    nq = n // (batch * tq)
    dh = MEM_H * MEM_DH
    return pl.pallas_call(
        _mem_attn_kernel,
        grid=(batch, nq),
        in_specs=[pl.BlockSpec((tq, D_MODEL), lambda b, i: (b * nq + i, 0)),
                  pl.BlockSpec((tq, dh), lambda b, i: (b * nq + i, 0)),
                  pl.BlockSpec((N_MEM, dh), lambda b, i: (kblk0 + b, 0)),
                  pl.BlockSpec((N_MEM, dh), lambda b, i: (vblk0 + b, vcol)),
                  pl.BlockSpec(wo.shape, lambda b, i: (0, 0))],
        out_specs=pl.BlockSpec((tq, D_MODEL), lambda b, i: (b * nq + i, 0)),
        out_shape=jax.ShapeDtypeStruct((n, D_MODEL), F32),
        compiler_params=_params("arbitrary", "arbitrary"),
        name="mem_attn",
    )(x, q, mk, mv, wo)


def _top16(s, payload=None):
    r = s.shape[0]
    iota = lax.broadcasted_iota(jnp.int32, s.shape, 0)
    vals, idxs = [], []
    for _ in range(PEER_TOPK):
        m = jnp.max(s, axis=0, keepdims=True)
        idx = jnp.min(jnp.where(s == m, iota, r), axis=0, keepdims=True)
        hit = iota == idx
        vals.append(m)
        if payload is None:
            idxs.append(idx)
        else:
            idxs.append(jnp.max(jnp.where(hit, payload, -1), axis=0, keepdims=True))
        s = jnp.where(hit, -jnp.inf, s)
    return vals, idxs


_PAIR_ROWS = [PEER_TOPK // (a + 1) for a in range(PEER_TOPK)]


def _peer_topk_kernel(q_ref, sk1_ref, sk2_ref, eid_ref, gate_ref):
    tm = q_ref.shape[0]
    row8 = lax.broadcasted_iota(jnp.int32, (SUBLANES, tm), 0)
    n_wide = sum(1 for r in _PAIR_ROWS if r > 1)
    for h in range(PEER_H):
        qh = q_ref[:, h * LANES:(h + 1) * LANES]
        s1 = _dot_nt(sk1_ref[h], qh)
        s2 = _dot_nt(sk2_ref[h], qh)
        v1, i1 = _top16(s1)
        v2, i2 = _top16(s2)
        v2a = jnp.concatenate(v2, axis=0)
        i2a = jnp.concatenate(i2, axis=0)
        cand, cidx = [], []
        for a in range(n_wide):
            nb = _PAIR_ROWS[a]
            rows = -(-nb // SUBLANES) * SUBLANES
            va = jnp.broadcast_to(v1[a], (rows, tm)) + v2a[:rows]
            ia = jnp.broadcast_to(i1[a], (rows, tm)) * PEER_N_KEYS + i2a[:rows]
            if nb < rows:
                va = jnp.where(row8 < nb, va, -jnp.inf)
            cand.append(va)
            cidx.append(ia)
        cand.append(jnp.concatenate(v1[n_wide:], axis=0) + jnp.broadcast_to(v2[0], (PEER_TOPK - n_wide, tm)))
        cidx.append(jnp.concatenate(i1[n_wide:], axis=0) * PEER_N_KEYS
                    + jnp.broadcast_to(i2[0], (PEER_TOPK - n_wide, tm)))
        sc, eid = _top16(jnp.concatenate(cand, axis=0), jnp.concatenate(cidx, axis=0))
        sc = jnp.concatenate(sc, axis=0)
        e = jnp.exp(sc - sc[0:1])
        gate_ref[h * PEER_TOPK:(h + 1) * PEER_TOPK, :] = e / jnp.sum(e, axis=0, keepdims=True)
        eid_ref[h * PEER_TOPK:(h + 1) * PEER_TOPK, :] = jnp.concatenate(eid, axis=0)


def peer_topk(q, sk1, sk2, tm=128):
    n = q.shape[0]
    rows = PEER_H * PEER_TOPK
    return pl.pallas_call(
        _peer_topk_kernel,
        grid=(n // tm,),
        in_specs=[pl.BlockSpec((tm, D_MODEL), lambda i: (i, 0)),
                  pl.BlockSpec(sk1.shape, lambda i: (0, 0, 0)),
                  pl.BlockSpec(sk2.shape, lambda i: (0, 0, 0))],
        out_specs=[pl.BlockSpec((rows, tm), lambda i: (0, i)),
                   pl.BlockSpec((rows, tm), lambda i: (0, i))],
        out_shape=[jax.ShapeDtypeStruct((rows, n), jnp.int32),
                   jax.ShapeDtypeStruct((rows, n), F32)],
        compiler_params=_params("arbitrary"),
        name="peer_topk",
    )(q, sk1, sk2)


PEER_ROWS = PEER_H * PEER_TOPK
PEER_TOK_BLOCK = 128
PEER_SLOTS = 8
PEER_CHUNKS = D_MODEL // LANES
PEER_TAB_BLOCK = 512
U32 = jnp.uint32


def _gelu_tanh(x):
    return 0.5 * x * (1.0 + jnp.tanh(math.sqrt(2.0 / math.pi) * (x + 0.044715 * x * x * x)))


def _peer_table_kernel(u_ref, v_ref, o_ref):
    eb = u_ref.shape[0]
    quarter = PEER_CHUNKS // 2

    def pack(a, b):
        wa = pltpu.bitcast(a.astype(BF16).astype(F32), U32)
        wb = pltpu.bitcast(b.astype(BF16).astype(F32), U32)
        return wa | (wb >> 16)

    for c in range(quarter):
        lo, mid, hi = 2 * c * LANES, (2 * c + 1) * LANES, (2 * c + 2) * LANES
        o_ref[pl.ds(c, eb, stride=PEER_CHUNKS), :] = pack(u_ref[:, lo:mid], u_ref[:, mid:hi])
        o_ref[pl.ds(quarter + c, eb, stride=PEER_CHUNKS), :] = pack(v_ref[:, lo:mid], v_ref[:, mid:hi])


def peer_table(u2, v2):
    rows = u2.shape[0]
    eb = PEER_TAB_BLOCK
    return pl.pallas_call(
        _peer_table_kernel,
        grid=(rows // eb,),
        in_specs=[pl.BlockSpec((eb, D_MODEL), lambda i: (i, 0)), pl.BlockSpec((eb, D_MODEL), lambda i: (i, 0))],
        out_specs=pl.BlockSpec((eb * PEER_CHUNKS, LANES), lambda i: (i, 0)),
        out_shape=jax.ShapeDtypeStruct((rows * PEER_CHUNKS, LANES), U32),
        compiler_params=_params("arbitrary"),
        name="peer_table",
    )(u2, v2)


def _peer_gather_kernel(eid_ref, gate_ref, h_ref, x_ref, tab_ref, o_ref, *scratch, nsteps):
    bufs, sem = scratch[:PEER_SLOTS], scratch[PEER_SLOTS]
    ntok = x_ref.shape[0]
    lookahead = PEER_SLOTS - 1
    step = pl.program_id(0)
    half = PEER_CHUNKS // 2

    def issue(tok, slot):
        for k in range(PEER_ROWS):
            row0 = pl.multiple_of(eid_ref[0, tok, k] * PEER_CHUNKS, PEER_CHUNKS)
            pltpu.make_async_copy(tab_ref.at[pl.ds(row0, PEER_CHUNKS), :], bufs[slot].at[:, k, :],
                                  sem.at[slot]).start(priority=k % 2)

    def wait(slot):
        pltpu.make_async_copy(bufs[slot], bufs[slot], sem.at[slot]).wait()

    lane_row = lax.broadcasted_iota(jnp.int32, (1, ntok), 1)

    def unpack(words):
        return pltpu.bitcast(words & jnp.uint32(0xFFFF0000), F32), pltpu.bitcast(words << 16, F32)

    def compute(tok, slot):
        buf = bufs[slot]
        hrow = h_ref[pl.ds(tok, 1), :]
        acc = None
        for c in range(half):
            hi, lo = unpack(buf[c])
            t = (hi * hrow[:, 2 * c * LANES:(2 * c + 1) * LANES]
                 + lo * hrow[:, (2 * c + 1) * LANES:(2 * c + 2) * LANES])
            acc = t if acc is None else acc + t
        act = jnp.sum(acc, axis=1, keepdims=True)
        gcol = jnp.sum(jnp.where(lane_row == tok, gate_ref[...], 0.0), axis=1, keepdims=True)
        w = gcol * _gelu_tanh(act)
        ys = []
        for c in range(half):
            hi, lo = unpack(buf[half + c])
            ys.append(jnp.sum(hi * w, axis=0, keepdims=True))
            ys.append(jnp.sum(lo * w, axis=0, keepdims=True))
        o_ref[pl.ds(tok, 1), :] = x_ref[pl.ds(tok, 1), :] + jnp.concatenate(ys, axis=1)

    @pl.when(step == 0)
    def _():
        for t in range(lookahead):
            issue(t, t)

    def group(g, carry):
        for j in range(PEER_SLOTS):
            tok = g * PEER_SLOTS + j
            wait(j)
            issue(tok + lookahead, (j + lookahead) % PEER_SLOTS)
            compute(tok, j)
        return carry

    lax.fori_loop(0, ntok // PEER_SLOTS, group, 0)

    @pl.when(step == nsteps - 1)
    def _():
        for t in range(lookahead):
            wait(t)


def peer_gather(eid, gate_t, hn, x, table):
    n = x.shape[0]
    tb = PEER_TOK_BLOCK
    nsteps = n // tb
    eid3 = eid.reshape(nsteps, tb, PEER_ROWS)
    head = jnp.concatenate([eid3[1:, :PEER_SLOTS], eid3[-1:, :PEER_SLOTS]], axis=0)
    eid_ext = jnp.concatenate([eid3, head], axis=1)
    return pl.pallas_call(
        functools.partial(_peer_gather_kernel, nsteps=nsteps),
        grid=(nsteps,),
        in_specs=[pl.BlockSpec((1, tb + PEER_SLOTS, PEER_ROWS), lambda i: (i, 0, 0), memory_space=pltpu.SMEM),
                  pl.BlockSpec((PEER_ROWS, tb), lambda i: (0, i)),
                  pl.BlockSpec((tb, D_MODEL), lambda i: (i, 0)),
                  pl.BlockSpec((tb, D_MODEL), lambda i: (i, 0)),
                  pl.BlockSpec(memory_space=pl.ANY)],
        out_specs=pl.BlockSpec((tb, D_MODEL), lambda i: (i, 0)),
        out_shape=jax.ShapeDtypeStruct((n, D_MODEL), F32),
        scratch_shapes=[pltpu.VMEM((PEER_CHUNKS, PEER_ROWS, LANES), U32)] * PEER_SLOTS
                       + [pltpu.SemaphoreType.DMA((PEER_SLOTS,))],
        compiler_params=_params("arbitrary"),
        name="peer_gather",
    )(eid_ext, gate_t, hn, x, table)


def _rope_tables(pos):
    half = MLA_ROPE // 2
    inv = jnp.exp(-math.log(ROPE_THETA) * jnp.arange(half, dtype=F32) / half)
    ang = pos.astype(F32)[:, None] * inv[None, :]
    cos, sin = jnp.cos(ang), jnp.sin(ang)
    zeros = jnp.zeros_like(cos)
    cos_t = jnp.concatenate([cos, cos, zeros, zeros], axis=1)
    sin_t = jnp.concatenate([-sin, sin, zeros, zeros], axis=1)
    return cos_t, sin_t


def _mix_and_ffn(x, layer, wts, seq_fn):
    proj = norm_matmul(x, wts["ln_mix_g"][layer], wts["w_in"][layer])
    og, olat, extras = seq_fn(proj)
    x = outproj(x, og, olat, wts["w_uv"][layer], wts["w_out"][layer])
    return x, proj, extras


def _peer(x, layer, wts):
    q, hn = norm_matmul(x, wts["ln_peer_g"][layer], wts["peer_wq"][layer], emit_normed=True)
    eid_t, gate_t = peer_topk(q, wts["sk1"][layer], wts["sk2"][layer])
    eid = eid_t.T + layer * wts["n_experts"]
    return peer_gather(eid, gate_t, hn, x, wts["peer_tab"])


def kernel(x_prompt, x_sample, mem_prompt, cache_mla_ckv, cache_mla_krope, cache_mem_k, cache_mem_v, state_gdn, state_gdn_conv, page_table, ln_mix_g, w_in, gdn_conv_w, gdn_a_log, gdn_dt_bias, gdn_norm_g, mla_q_norm_g, mla_w_uq, mla_kv_norm_g, mla_w_uk, mla_w_uv, w_out, ln_mem_g, mem_norm_g, mem_wq, mem_wk, mem_wv, mem_wo, ln_peer_g, peer_wq, peer_subkeys, peer_u, peer_v, final_norm_g):
    depth = w_in.shape[0]
    b_p, s_len, _ = x_prompt.shape
    b_s, t_new, _ = x_sample.shape
    past_len = page_table.shape[1] * PAGE_SIZE
    n_p, n_s = b_p * s_len, b_s * t_new

    c_z = GDN_QKV + GDN_W
    c_a, c_b, c_cq = c_z, c_z + GDN_H, c_z + 2 * GDN_H
    c_ckv = c_cq + MLA_Q_LORA
    c_kr = c_ckv + MLA_KV_LORA
    w_in_r = jnp.concatenate(
        [w_in[:, :, :c_z], w_in[:, :, c_ckv:c_kr], w_in[:, :, c_cq:c_ckv], w_in[:, :, c_kr:c_kr + MLA_ROPE],
         w_in[:, :, c_a:c_a + GDN_H], w_in[:, :, c_b:c_b + GDN_H],
         jnp.zeros((depth, D_MODEL, LANES - MLA_ROPE - 2 * GDN_H), F32)], axis=2).astype(BF16)
    assert w_in_r.shape[2] == D_PROJ
    wuq_nope = mla_w_uq[:, :, :, :MLA_NOPE].reshape(depth, MLA_Q_LORA, MLA_H * MLA_NOPE)
    wuq_rope = jnp.pad(mla_w_uq[:, :, :, MLA_NOPE:], ((0, 0), (0, 0), (0, 0), (0, LANES - MLA_ROPE)))
    wuq = jnp.concatenate([wuq_nope, wuq_rope.reshape(depth, MLA_Q_LORA, MLA_H * LANES)], axis=2).astype(BF16)
    wuk = jnp.transpose(mla_w_uk, (0, 2, 3, 1)).astype(BF16)
    wuv = jnp.transpose(mla_w_uv, (0, 2, 1, 3)).astype(BF16)
    zk = jnp.zeros((depth, PEER_H, PEER_N_KEYS, LANES // 2), F32)
    wts = {
        "ln_mix_g": ln_mix_g, "w_in": w_in_r, "w_uv": wuv, "w_out": w_out.astype(BF16),
        "ln_peer_g": ln_peer_g, "peer_wq": peer_wq.reshape(depth, D_MODEL, PEER_H * LANES).astype(BF16),
        "sk1": jnp.concatenate([peer_subkeys[:, :, 0], zk], axis=-1).astype(BF16),
        "sk2": jnp.concatenate([zk, peer_subkeys[:, :, 1]], axis=-1).astype(BF16),
        "peer_tab": peer_table(peer_u.reshape(-1, D_MODEL), peer_v.reshape(-1, D_MODEL)),
        "n_experts": peer_u.shape[1],
    }
    mem_wkv = jnp.concatenate([mem_wk, mem_wv], axis=2).astype(BF16)
    mem_wq_b = mem_wq.astype(BF16)
    mem_wo_b = mem_wo.astype(BF16)
    dh = MEM_H * MEM_DH
    cache_kr_t = jnp.swapaxes(cache_mla_krope, 2, 3)
    cache_k2 = cache_mem_k.reshape(depth * b_s * N_MEM, dh)
    cache_v2 = cache_mem_v.reshape(depth * b_s * N_MEM, dh)

    cos_p, sin_p = _rope_tables(jnp.arange(s_len))
    cos_p, sin_p = jnp.tile(cos_p, (b_p, 1)), jnp.tile(sin_p, (b_p, 1))
    cos_s, sin_s = _rope_tables(past_len + jnp.arange(t_new))
    cos_s, sin_s = jnp.tile(cos_s, (b_s, 1)), jnp.tile(sin_s, (b_s, 1))

    xp = x_prompt.reshape(n_p, D_MODEL)
    xs = x_sample.reshape(n_s, D_MODEL)
    mem2 = mem_prompt.reshape(b_p * N_MEM, D_MODEL)
    zero_cbuf = jnp.zeros((b_p, SUBLANES, GDN_QKV), F32)
    zero_state = jnp.zeros((b_p, GDN_H, GDN_DK, LANES), F32)
    tpad = SUBLANES

    outs = {k: [] for k in ("p_ckv", "p_kr", "p_gdn", "p_conv", "p_mk", "p_mv", "s_ckv", "s_kr", "s_gdn", "s_conv")}
    for l in range(depth):
        gdn_w = (gdn_conv_w[l], gdn_a_log[l], gdn_dt_bias[l], gdn_norm_g[l])
        mla_w = (mla_q_norm_g[l], mla_kv_norm_g[l], wuq[l], wuk[l])

        def prompt_seq(proj):
            og, s_new = gdn(proj, zero_cbuf, zero_state, *gdn_w, batch=b_p, rb=CHUNK, valid=CHUNK)
            qcat, kvcat, ckv, kr = mla_prep(proj, cos_p, sin_p, *mla_w)
            olat = mla_flash(qcat, kvcat, batch=b_p, seq=s_len)
            return og, olat, (s_new, ckv, kr)

        xp, proj, (sp, ckvp, krp) = _mix_and_ffn(xp, l, wts, prompt_seq)
        outs["p_gdn"].append(sp)
        outs["p_ckv"].append(ckvp.reshape(b_p, s_len, MLA_KV_LORA))
        outs["p_kr"].append(krp.reshape(b_p, s_len, MLA_ROPE))
        qkv_pre = proj[:, :GDN_QKV].reshape(b_p, s_len, GDN_QKV)
        outs["p_conv"].append(qkv_pre[:, s_len - (GDN_CONV - 1):])
        mkv = norm_matmul(mem2, mem_norm_g[l], mem_wkv[l])
        outs["p_mk"].append(mkv[:, :dh].reshape(b_p, N_MEM, MEM_H, MEM_DH))
        outs["p_mv"].append(mkv[:, dh:].reshape(b_p, N_MEM, MEM_H, MEM_DH))
        qm = norm_matmul(xp, ln_mem_g[l], mem_wq_b[l])
        xp = mem_attn(xp, qm, mkv, mkv, mem_wo_b[l], batch=b_p, tq=256, kblk0=0, vblk0=0, vcol=1)
        xp = _peer(xp, l, wts)

        def sample_seq(proj):
            proj_pad = jnp.pad(proj.reshape(b_s, t_new, D_PROJ), ((0, 0), (0, tpad - t_new), (0, 0)))
            cbuf = jnp.pad(state_gdn_conv[l], ((0, 0), (SUBLANES - (GDN_CONV - 1), 0), (0, 0)))
            og, s_new = gdn(proj_pad.reshape(b_s * tpad, D_PROJ), cbuf, state_gdn[l], *gdn_w,
                            batch=b_s, rb=tpad, valid=t_new)
            og = og.reshape(b_s, tpad, -1)[:, :t_new].reshape(n_s, -1)
            qcat, kvcat, ckv, kr = mla_prep(proj, cos_s, sin_s, *mla_w)
            olat = mla_paged(qcat, kvcat, cache_mla_ckv, cache_kr_t, page_table, layer=l, t_new=t_new)
            return og, olat, (s_new, ckv, kr)

        xs, proj_s, (ss, ckvs, krs) = _mix_and_ffn(xs, l, wts, sample_seq)
        outs["s_gdn"].append(ss)
        outs["s_ckv"].append(ckvs.reshape(b_s, t_new, MLA_KV_LORA))
        outs["s_kr"].append(krs.reshape(b_s, t_new, MLA_ROPE))
        qkv_pre = proj_s[:, :GDN_QKV].reshape(b_s, t_new, GDN_QKV)
        outs["s_conv"].append(qkv_pre[:, t_new - (GDN_CONV - 1):])
        qm = norm_matmul(xs, ln_mem_g[l], mem_wq_b[l])
        xs_pad = jnp.pad(xs.reshape(b_s, t_new, D_MODEL), ((0, 0), (0, tpad - t_new), (0, 0)))
        qm_pad = jnp.pad(qm.reshape(b_s, t_new, dh), ((0, 0), (0, tpad - t_new), (0, 0)))
        xs_pad = mem_attn(xs_pad.reshape(b_s * tpad, D_MODEL), qm_pad.reshape(b_s * tpad, dh), cache_k2, cache_v2,
                          mem_wo_b[l], batch=b_s, tq=tpad, kblk0=l * b_s, vblk0=l * b_s, vcol=0)
        xs = xs_pad.reshape(b_s, tpad, D_MODEL)[:, :t_new].reshape(n_s, D_MODEL)
        xs = _peer(xs, l, wts)

    y_prompt = final_norm(xp, final_norm_g).reshape(b_p, s_len, D_MODEL)
    y_sample = final_norm(xs, final_norm_g).reshape(b_s, t_new, D_MODEL)
    st = lambda k: jnp.stack(outs[k])
    return (y_prompt, y_sample, st("p_ckv"), st("p_kr"), st("p_gdn"), st("p_conv"), st("p_mk"), st("p_mv"),
            st("s_ckv"), st("s_kr"), st("s_gdn"), st("s_conv"))
```
